```python
import math
import jax, jax.numpy as jnp
from jax import lax
import numpy as np

D_MODEL = 1024
BATCH = 1
SEQ = 16384
DEPTH = 2
DEC_BATCH = 32
DEC_SEQ = 4
PAST_LEN = 16384
PAGE_SIZE = 128

GLA_HEADS = 4
GLA_DK = D_MODEL // 2 // GLA_HEADS
GLA_DV = D_MODEL // GLA_HEADS
GLA_RANK = 16
GLA_TAU = 16.0
GLA_CHUNK = 64
QK_A = GLA_HEADS * GLA_DK
V_A = GLA_HEADS * GLA_DV
DIL_PAIRS = ((128, 1), (512, 4), (2048, 16))
N_GROUPS = len(DIL_PAIRS)
DIL_HEADS = 4
DIL_HD = 128
DIL_STEPS = 128
DIL_OUT = DIL_HEADS * DIL_HD
ROPE_THETA = 500000.0
ROPE_DIM = DIL_HD // 4
D_FF = 4 * D_MODEL
CONV_W = 3
EPS = 1e-6
IN_SIZES = (QK_A, QK_A, V_A, V_A, GLA_RANK, 3 * N_GROUPS * DIL_OUT, 2 * D_MODEL)
D_IN = QK_A * 2 + V_A * 2 + GLA_RANK + 3 * N_GROUPS * DIL_OUT + 2 * D_MODEL

kernel_name = "hybrid_gla_dilated_swa_convffn_step"


def rmsnorm(x, w):
    xf = x.astype(jnp.float32)
    y = xf * lax.rsqrt(jnp.mean(xf * xf, axis=-1, keepdims=True) + EPS)
    return (y * w.astype(jnp.float32)).astype(x.dtype)


def rope(x, pos):
    half = ROPE_DIM // 2
    inv = ROPE_THETA ** (-jnp.arange(half, dtype=jnp.float32) * 2.0 / ROPE_DIM)
    ang = pos[:, None] * inv[None, :]
    ang = ang.reshape((ang.shape[0],) + (1,) * (x.ndim - 3) + (half,))
    cos, sin = jnp.cos(ang), jnp.sin(ang)
    xf = x.astype(jnp.float32)
    x1, x2, rest = xf[..., :half], xf[..., half:ROPE_DIM], xf[..., ROPE_DIM:]
    out = jnp.concatenate([x1 * cos - x2 * sin, x2 * cos + x1 * sin, rest], axis=-1)
    return out.astype(x.dtype)


def gla_scan(q, k, v, log_a, s0):
    B, T = q.shape[:2]
    c = math.gcd(T, GLA_CHUNK)
    n = T // c

    def chunks(a):
        a = a.astype(jnp.float32)
        return jnp.moveaxis(a.reshape((B, n, c) + a.shape[2:]), 1, 0)

    causal = jnp.tril(jnp.ones((c, c), dtype=bool))[None, :, :, None, None]

    def step(state, inp):
        qc, kc, vc, gc = inp
        b = jnp.cumsum(gc, axis=1)
        diff = b[:, :, None] - b[:, None, :]
        decay = jnp.exp(jnp.where(causal, diff, -jnp.inf))
        attn = jnp.einsum('bthk,bshk,btshk->btsh', qc, kc, decay)
        o = jnp.einsum('btsh,bshv->bthv', attn, vc) + jnp.einsum('bthk,bhkv->bthv', qc * jnp.exp(b), state)
        b_last = b[:, -1]
        k_dec = kc * jnp.exp(b_last[:, None] - b)
        new_state = state * jnp.exp(b_last)[..., None] + jnp.einsum('bshk,bshv->bhkv', k_dec, vc)
        return new_state, o

    state, o = lax.scan(step, s0.astype(jnp.float32), (chunks(q), chunks(k), chunks(v), chunks(log_a)))
    o = jnp.moveaxis(o, 0, 1).reshape(B, T, q.shape[2], v.shape[-1])
    return o, state


def dilated_prompt(q, k, v, dil):
    B, S, H, Dh = q.shape
    L = S // dil
    Lp = -(-L // DIL_STEPS) * DIL_STEPS
    nb = Lp // DIL_STEPS

    def streams(a):
        a = a.reshape(B, L, dil, H, Dh).transpose(0, 2, 1, 3, 4)
        a = jnp.pad(a, ((0, 0), (0, 0), (0, Lp - L), (0, 0), (0, 0)))
        return a.reshape(B, dil, nb, DIL_STEPS, H, Dh)

    def with_prev(a):
        prev = jnp.pad(a, ((0, 0), (0, 0), (1, 0), (0, 0), (0, 0), (0, 0)))[:, :, :-1]
        return jnp.concatenate([prev, a], axis=3)

    qs = streams(q)
    kk = with_prev(streams(k))
    vv = with_prev(streams(v))
    qi = jnp.arange(DIL_STEPS)[:, None] + DIL_STEPS
    ki = jnp.arange(2 * DIL_STEPS)[None, :]
    dist = qi - ki
    band = (dist >= 0) & (dist <= DIL_STEPS)
    blk = jnp.arange(nb)[:, None, None]
    valid = band[None] & ((blk > 0) | (ki[None] >= DIL_STEPS))
    s = jnp.einsum('brnqhd,brnkhd->brnhqk', qs, kk).astype(jnp.float32) * (DIL_HD ** -0.5)
    s = jnp.where(valid[None, None, :, None], s, -jnp.inf)
    lse = jax.nn.logsumexp(s, axis=-1)
    p = jnp.exp(s - lse[..., None])
    o = jnp.einsum('brnhqk,brnkhd->brnqhd', p, vv.astype(jnp.float32))
    o = o.reshape(B, dil, Lp, H, Dh)[:, :, :L].transpose(0, 2, 1, 3, 4).reshape(B, S, H, Dh)
    lse = lse.transpose(0, 1, 2, 4, 3).reshape(B, dil, Lp, H)[:, :, :L]
    lse = lse.transpose(0, 2, 1, 3).reshape(B, S, H)
    return o, lse


def dilated_sample(q, k_all, v_all, dil):
    T = q.shape[1]
    W = k_all.shape[1] - T
    j = jnp.arange(DIL_STEPS + 1)
    idx = W + jnp.arange(T)[:, None] - j[None, :] * dil
    valid = idx >= 0
    idx = jnp.maximum(idx, 0)
    kg = k_all[:, idx]
    vg = v_all[:, idx]
    s = jnp.einsum('bthd,btjhd->bthj', q, kg).astype(jnp.float32) * (DIL_HD ** -0.5)
    s = jnp.where(valid[None, :, None, :], s, -jnp.inf)
    lse = jax.nn.logsumexp(s, axis=-1)
    p = jnp.exp(s - lse[..., None])
    o = jnp.einsum('bthj,btjhd->bthd', p, vg.astype(jnp.float32))
    return o, lse


def layer(x, pos, past, lw):
    (w_in, w_gate_up, b_gate, gla_norm, w_branch_a, w_branch_b, w_out,
     norm_pre_mix, norm_post_mix, norm_pre_ffn, norm_post_ffn,
     w_ffn_up, w_conv, b_conv, w_ffn_down) = lw
    B, T, _ = x.shape
    dt = x.dtype
    xn = rmsnorm(x, norm_pre_mix)
    z = xn @ w_in
    split_points = np.cumsum(IN_SIZES)[:-1].tolist()
    q_a, k_a, v_a, r_a, a_low, qkv_b, gate_logits = jnp.split(z, split_points, axis=-1)

    q_a = q_a.reshape(B, T, GLA_HEADS, GLA_DK) * (GLA_DK ** -0.5)
    k_a = k_a.reshape(B, T, GLA_HEADS, GLA_DK)
    v_a = v_a.reshape(B, T, GLA_HEADS, GLA_DV)
    log_a = jax.nn.log_sigmoid((a_low @ w_gate_up + b_gate).astype(jnp.float32)) / GLA_TAU
    log_a = log_a.reshape(B, T, GLA_HEADS, GLA_DK)
    if past is None:
        s0 = jnp.zeros((B, GLA_HEADS, GLA_DK, GLA_DV), jnp.float32)
    else:
        s0 = past[0]
    o_a, gla_state = gla_scan(q_a, k_a, v_a, log_a, s0)
    o_a = rmsnorm(o_a, gla_norm) * jax.nn.silu(r_a.reshape(B, T, GLA_HEADS, GLA_DV).astype(jnp.float32))
    o_a = o_a.reshape(B, T, V_A).astype(dt)

    qkv_b = qkv_b.reshape(B, T, 3, N_GROUPS, DIL_HEADS, DIL_HD)
    q_b = rope(qkv_b[:, :, 0], pos)
    k_b = rope(qkv_b[:, :, 1], pos)
    v_b = qkv_b[:, :, 2]
    outs, lses, bufs = [], [], []
    for g, (win, dil) in enumerate(DIL_PAIRS):
        kv_new = jnp.stack([k_b[:, :, g], v_b[:, :, g]], axis=2)
        if past is None:
            o_g, lse_g = dilated_prompt(q_b[:, :, g], k_b[:, :, g], v_b[:, :, g], dil)
            bufs.append(kv_new[:, T - min(win, T):])
        else:
            kv_all = jnp.concatenate([past[1 + g].astype(kv_new.dtype), kv_new], axis=1)
            o_g, lse_g = dilated_sample(q_b[:, :, g], kv_all[:, :, 0], kv_all[:, :, 1], dil)
            bufs.append(kv_all[:, T:])
        outs.append(o_g)
        lses.append(lse_g)
    w_grp = jax.nn.softmax(jnp.stack(lses, axis=0), axis=0)
    o_b = jnp.sum(w_grp[..., None] * jnp.stack(outs, axis=0), axis=0).reshape(B, T, DIL_OUT).astype(dt)

    g_a, g_b = jnp.split(jax.nn.sigmoid(gate_logits), 2, axis=-1)
    merged = g_a * (o_a @ w_branch_a) + g_b * (o_b @ w_branch_b)
    h = x + rmsnorm(merged @ w_out, norm_post_mix)

    hn = rmsnorm(h, norm_pre_ffn)
    gate_pre, val = jnp.split(hn @ w_ffn_up, 2, axis=-1)
    if past is None:
        prev = jnp.zeros((B, CONV_W - 1, D_FF), gate_pre.dtype)
    else:
        prev = past[4].astype(gate_pre.dtype)
    gcat = jnp.concatenate([prev, gate_pre], axis=1)
    conv = lax.conv_general_dilated(gcat, w_conv[:, None, :].astype(gcat.dtype), window_strides=(1,),
                                    padding='VALID', dimension_numbers=('NWC', 'WIO', 'NWC'),
                                    feature_group_count=D_FF) + b_conv
    act = jax.nn.gelu(conv, approximate=True) * val
    out = h + rmsnorm(act @ w_ffn_down, norm_post_ffn)
    new_state = (gla_state, bufs[0], bufs[1], bufs[2], gcat[:, T:])
    return out, new_state


def setup_inputs(seed: int = 0) -> dict:
    key = jax.random.key(seed)
    ks = jax.random.split(key, 24)

    def nrm(k, shape, scale):
        return jax.random.normal(k, shape, jnp.float32) * scale

    def buf(k, win):
        return nrm(k, (DEPTH, DEC_BATCH, min(win, PAST_LEN), 2, DIL_HEADS, DIL_HD), 1.0)

    return {
        'x_prompt': nrm(ks[0], (BATCH, SEQ, D_MODEL), 1.0),
        'x_sample': nrm(ks[1], (DEC_BATCH, DEC_SEQ, D_MODEL), 1.0),
        'state_gla': nrm(ks[2], (DEPTH, DEC_BATCH, GLA_HEADS, GLA_DK, GLA_DV), 1.0),
        'cache_dil_w128': buf(ks[3], DIL_PAIRS[0][0]),
        'cache_dil_w512': buf(ks[4], DIL_PAIRS[1][0]),
        'cache_dil_w2048': buf(ks[5], DIL_PAIRS[2][0]),
        'state_ffn_conv': nrm(ks[6], (DEPTH, DEC_BATCH, CONV_W - 1, D_FF), 1.0),
        'w_in': nrm(ks[7], (DEPTH, D_MODEL, D_IN), D_MODEL ** -0.5),
        'w_gate_up': nrm(ks[8], (DEPTH, GLA_RANK, QK_A), GLA_RANK ** -0.5),
        'b_gate': nrm(ks[9], (DEPTH, QK_A), 0.1),
        'gla_norm': 1.0 + nrm(ks[10], (DEPTH, GLA_DV), 0.05),
        'w_branch_a': nrm(ks[11], (DEPTH, V_A, D_MODEL), V_A ** -0.5),
        'w_branch_b': nrm(ks[12], (DEPTH, DIL_OUT, D_MODEL), DIL_OUT ** -0.5),
        'w_out': nrm(ks[13], (DEPTH, D_MODEL, D_MODEL), D_MODEL ** -0.5),
        'norm_pre_mix': 1.0 + nrm(ks[14], (DEPTH, D_MODEL), 0.05),
        'norm_post_mix': 1.0 + nrm(ks[15], (DEPTH, D_MODEL), 0.05),
        'norm_pre_ffn': 1.0 + nrm(ks[16], (DEPTH, D_MODEL), 0.05),
        'norm_post_ffn': 1.0 + nrm(ks[17], (DEPTH, D_MODEL), 0.05),
        'w_ffn_up': nrm(ks[18], (DEPTH, D_MODEL, 2 * D_FF), D_MODEL ** -0.5),
        'w_conv': nrm(ks[19], (DEPTH, CONV_W, D_FF), CONV_W ** -0.5),
        'b_conv': nrm(ks[20], (DEPTH, D_FF), 0.02),
        'w_ffn_down': nrm(ks[21], (DEPTH, D_FF, D_MODEL), D_FF ** -0.5),
    }


def reference(x_prompt, x_sample, state_gla, cache_dil_w128, cache_dil_w512, cache_dil_w2048,
              state_ffn_conv, w_in, w_gate_up, b_gate, gla_norm, w_branch_a, w_branch_b, w_out,
              norm_pre_mix, norm_post_mix, norm_pre_ffn, norm_post_ffn, w_ffn_up, w_conv, b_conv,
              w_ffn_down):
    weights = (w_in, w_gate_up, b_gate, gla_norm, w_branch_a, w_branch_b, w_out,
               norm_pre_mix, norm_post_mix, norm_pre_ffn, norm_post_ffn,
               w_ffn_up, w_conv, b_conv, w_ffn_down)
    pos_p = jnp.arange(x_prompt.shape[1], dtype=jnp.float32)
    pos_s = PAST_LEN + jnp.arange(x_sample.shape[1], dtype=jnp.float32)
    hp, hs = x_prompt, x_sample
    new_p, new_s = [], []
    for i in range(DEPTH):
        lw = [w[i] for w in weights]
        hp, st_p = layer(hp, pos_p, None, lw)
        past = (state_gla[i], cache_dil_w128[i], cache_dil_w512[i], cache_dil_w2048[i], state_ffn_conv[i])
        hs, st_s = layer(hs, pos_s, past, lw)
        new_p.append(st_p)
        new_s.append(st_s)

    def stk(sts, j):
        return jnp.stack([s[j] for s in sts], axis=0)

    return (hp, hs,
            stk(new_p, 0), stk(new_p, 1), stk(new_p, 2), stk(new_p, 3), stk(new_p, 4),
            stk(new_s, 0), stk(new_s, 1), stk(new_s, 2), stk(new_s, 3), stk(new_s, 4))
```

```python
import functools

import numpy as np
import jax
import jax.numpy as jnp
from jax import lax
from jax.experimental import pallas as pl
from jax.experimental.pallas import tpu as pltpu

F32 = jnp.float32
BF16 = jnp.bfloat16

GLA_HEADS = 4
GLA_DK = 128
GLA_DV = 256
GLA_RANK = 16
GLA_TAU = 16.0
DIL_PAIRS = ((128, 1), (512, 4), (2048, 16))
N_GROUPS = len(DIL_PAIRS)
DIL_HEADS = 4
DIL_HD = 128
DIL_STEPS = 128
DIL_OUT = DIL_HEADS * DIL_HD
ROPE_THETA = 500000.0
ROPE_DIM = DIL_HD // 4
ROPE_HALF = ROPE_DIM // 2
CONV_W = 3
EPS = 1e-6
PAST_LEN = 16384

LANES = 128
SUBLANES = 8
VMEM_LIMIT = 56 * 1024 * 1024

CW = 512
Z_QA, Z_KA, Z_VA, Z_RA, Z_GA, Z_GB, Z_QB, Z_KB, Z_VB = 0, 1, 2, 4, 6, 8, 10, 13, 16
Z_BLOCKS = 19
SROWS = 8
GLA_C = 128
NEG = -1e30
COPY_ROWS = 32


def _params(sem):
    return pltpu.CompilerParams(dimension_semantics=sem, vmem_limit_bytes=VMEM_LIMIT)


def _resident(shape):
    nd = len(shape)
    return pl.BlockSpec(shape, lambda *_: (0,) * nd, pipeline_mode=pl.Buffered(1))


def _rms(x, w):
    ms = jnp.mean(x * x, axis=-1, keepdims=True)
    return x * lax.rsqrt(ms + EPS) * w


def _dot(a, b):
    return jnp.dot(a, b, preferred_element_type=F32)


def _dot_nt(a, b):
    return lax.dot_general(a, b, (((1,), (1,)), ((), ())), preferred_element_type=F32)


def _rope_heads(acc, cos_t, sin_t, lt):
    parts = []
    for h in range(DIL_HEADS):
        a = acc[:, h * LANES:(h + 1) * LANES]
        up = pltpu.roll(a, LANES - ROPE_HALF, axis=1)
        dn = pltpu.roll(a, ROPE_HALF, axis=1)
        parts.append(a * cos_t + jnp.where(lt, up, dn) * sin_t)
    return parts


def _inproj_kernel(x_ref, nw_ref, wz_ref, wal_ref, wgu_ref, bg_ref, cos_ref, sin_ref,
                   z_ref, la_ref):
    xn = _rms(x_ref[...], nw_ref[...]).astype(BF16)
    cos_t = cos_ref[...]
    sin_t = sin_ref[...]
    lt = lax.broadcasted_iota(jnp.int32, cos_t.shape, 1) < ROPE_HALF
    for c in range(Z_BLOCKS):
        acc = _dot(xn, wz_ref[:, c * CW:(c + 1) * CW])
        if c == Z_QA:
            acc = acc * (GLA_DK ** -0.5)
        elif Z_GA <= c < Z_QB:
            acc = jax.nn.sigmoid(acc)
        if Z_QB <= c < Z_VB:
            scale = (DIL_HD ** -0.5) if c < Z_KB else 1.0
            for h, p in enumerate(_rope_heads(acc, cos_t, sin_t, lt)):
                z_ref[:, c * CW + h * LANES:c * CW + (h + 1) * LANES] = (p * scale).astype(BF16)
        else:
            z_ref[:, c * CW:(c + 1) * CW] = acc.astype(BF16)
    al = _dot(xn, wal_ref[...]).astype(BF16)
    lg = _dot(al, wgu_ref[...]) + bg_ref[...]
    la_ref[...] = (jnp.minimum(lg, 0.0) - jnp.log1p(jnp.exp(-jnp.abs(lg)))) * (1.0 / GLA_TAU)


def _inproj(x, nw, wz, wal, wgu, bg, cos_t, sin_t, tm):
    m, d = x.shape
    nz = wz.shape[1]
    qk = wgu.shape[1]
    row = lambda w: pl.BlockSpec((tm, w), lambda i: (i, 0))
    return pl.pallas_call(
        _inproj_kernel,
        grid=(m // tm,),
        in_specs=[row(d), _resident((1, d)), _resident((d, nz)), _resident((d, LANES)),
                  _resident((LANES, qk)), _resident((1, qk)), row(LANES), row(LANES)],
        out_specs=[row(nz), row(qk)],
        out_shape=[jax.ShapeDtypeStruct((m, nz), BF16), jax.ShapeDtypeStruct((m, qk), F32)],
        compiler_params=_params(("parallel",)),
        name="inproj",
    )(x, nw, wz, wal, wgu, bg, cos_t, sin_t)


def _kvtail_kernel(x_ref, nw_ref, w_ref, cos_ref, sin_ref, o_ref):
    xn = _rms(x_ref[...], nw_ref[...]).astype(BF16)
    cos_t = cos_ref[...]
    sin_t = sin_ref[...]
    lt = lax.broadcasted_iota(jnp.int32, cos_t.shape, 1) < ROPE_HALF
    for c in range(2 * N_GROUPS):
        acc = _dot(xn, w_ref[:, c * CW:(c + 1) * CW])
        if c % 2 == 0:
            for h, p in enumerate(_rope_heads(acc, cos_t, sin_t, lt)):
                o_ref[:, c * CW + h * LANES:c * CW + (h + 1) * LANES] = p
        else:
            o_ref[:, c * CW:(c + 1) * CW] = acc


def _kvtail(x, nw, wkv, cos_t, sin_t, tm):
    m, d = x.shape
    n = wkv.shape[1]
    row = lambda w: pl.BlockSpec((tm, w), lambda i: (i, 0))
    return pl.pallas_call(
        _kvtail_kernel,
        grid=(m // tm,),
        in_specs=[row(d), _resident((1, d)), _resident((d, n)), row(LANES), row(LANES)],
        out_specs=row(n),
        out_shape=jax.ShapeDtypeStruct((m, n), F32),
        compiler_params=_params(("parallel",)),
        name="kvtail",
    )(x, nw, wkv, cos_t, sin_t)


@functools.lru_cache(maxsize=None)
def _gla_consts(C, nl, seg):
    x = np.arange(C)[:, None]
    i = np.arange(C)[None, :]
    same = (x // seg == i // seg) if seg else np.ones((C, C), bool)
    mats = [(i <= x) & same, (i > x) & same]
    masks = [x == i]
    for l in range(nl):
        m = 1 << l
        r = (x | (2 * m - 1)) - m
        mats.append((i > np.minimum(x, r)) & (i <= np.maximum(x, r)))
        masks.append((((x >> l) ^ (i >> l)) == 1) & (((x >> l) & 1) == 1))
    pall = np.concatenate(mats, 0).astype(np.float32)
    return pall, np.stack(masks).astype(np.float32)


def _gla_core(q, k, v, g, pall_ref, mask_ref, nl):
    C = q.shape[0]
    g_hi = g.astype(BF16)
    g_lo = (g - g_hi.astype(F32)).astype(BF16)
    ex = _dot(pall_ref[...], jnp.concatenate([g_hi, g_lo], axis=1))
    ex = ex[:, :GLA_DK] + ex[:, GLA_DK:]
    b = ex[0:C]
    aft = ex[C:2 * C]
    qf = q.astype(F32)
    kf = k.astype(F32)
    a = _dot_nt(q, k) * mask_ref[0]
    for l in range(nl):
        e = jnp.exp(ex[(2 + l) * C:(3 + l) * C])
        a = a + _dot_nt((qf * e).astype(BF16), (kf * e).astype(BF16)) * mask_ref[1 + l]
    o = _dot(a.astype(BF16), v)
    qe = (qf * jnp.exp(b)).astype(BF16)
    kdt = (kf * jnp.exp(aft)).T.astype(BF16)
    return o, qe, kdt, b.T


def _gla_prompt_kernel(q_ref, k_ref, v_ref, g_ref, pall_ref, mask_ref, o_ref, sf_ref, s_scr, *, nl):
    i = pl.program_id(0)

    @pl.when(i == 0)
    def _():
        s_scr[...] = jnp.zeros_like(s_scr)

    C = q_ref.shape[0]
    for h in range(GLA_HEADS):
        ks = slice(h * GLA_DK, (h + 1) * GLA_DK)
        vs = slice(h * GLA_DV, (h + 1) * GLA_DV)
        v = v_ref[:, vs]
        o, qe, kdt, bt = _gla_core(q_ref[:, ks], k_ref[:, ks], v, g_ref[:, ks], pall_ref, mask_ref, nl)
        s = s_scr[h]
        o_ref[:, vs] = (o + _dot(qe, s.astype(BF16))).astype(BF16)
        s_scr[h] = s * jnp.exp(bt[:, C - 1:C]) + _dot(kdt, v)

    @pl.when(i == pl.num_programs(0) - 1)
    def _():
        sf_ref[...] = s_scr[...]


def _gla_prompt(z, la):
    m = z.shape[0]
    C = GLA_C
    nl = C.bit_length() - 1
    pall, masks = _gla_consts(C, nl, 0)
    pall = jnp.asarray(pall, BF16)
    masks = jnp.asarray(masks, F32)
    qk = GLA_HEADS * GLA_DK
    va = GLA_HEADS * GLA_DV
    return pl.pallas_call(
        functools.partial(_gla_prompt_kernel, nl=nl),
        grid=(m // C,),
        in_specs=[pl.BlockSpec((C, qk), lambda i: (i, Z_QA)),
                  pl.BlockSpec((C, qk), lambda i: (i, Z_KA)),
                  pl.BlockSpec((C, va), lambda i: (i, Z_VA * CW // va)),
                  pl.BlockSpec((C, qk), lambda i: (i, 0)),
                  _resident(pall.shape), _resident(masks.shape)],
        out_specs=[pl.BlockSpec((C, va), lambda i: (i, 0)),
                   pl.BlockSpec((GLA_HEADS, GLA_DK, GLA_DV), lambda i: (0, 0, 0))],
        out_shape=[jax.ShapeDtypeStruct((m, va), BF16),
                   jax.ShapeDtypeStruct((GLA_HEADS, GLA_DK, GLA_DV), F32)],
        scratch_shapes=[pltpu.VMEM((GLA_HEADS, GLA_DK, GLA_DV), F32)],
        compiler_params=_params(("arbitrary",)),
        name="gla_prompt",
    )(z, z, z, la, pall, masks)


def _gla_sample_kernel(q_ref, k_ref, v_ref, g_ref, s0_ref, pall_ref, mask_ref, o_ref, sf_ref, *, nl, tvalid):
    C = q_ref.shape[0]
    nseq = C // SROWS
    row = lax.broadcasted_iota(jnp.int32, (C, 1), 0)
    g = jnp.where(row % SROWS < tvalid, g_ref[...], 0.0)
    v = v_ref[...]
    o, qe, kdt, bt = _gla_core(q_ref[...], k_ref[...], v, g, pall_ref, mask_ref, nl)
    zero_v = jnp.zeros_like(v)
    for j in range(nseq):
        mine = row // SROWS == j
        s = s0_ref[j, 0]
        o = o + jnp.where(mine, _dot(qe, s.astype(BF16)), 0.0)
        last = j * SROWS + SROWS - 1
        sf_ref[j, 0] = s * jnp.exp(bt[:, last:last + 1]) + _dot(kdt, jnp.where(mine, v, zero_v))
    o_ref[...] = o.astype(BF16)


def _gla_sample(z, la, s0, tvalid):
    m = z.shape[0]
    C = GLA_C
    nseq = C // SROWS
    nl = (tvalid - 1).bit_length()
    pall, masks = _gla_consts(C, nl, SROWS)
    pall = jnp.asarray(pall, BF16)
    masks = jnp.asarray(masks, F32)
    va = GLA_HEADS * GLA_DV
    sblk = (nseq, 1, GLA_DK, GLA_DV)
    return pl.pallas_call(
        functools.partial(_gla_sample_kernel, nl=nl, tvalid=tvalid),
        grid=(m // C, GLA_HEADS),
        in_specs=[pl.BlockSpec((C, GLA_DK), lambda c, h: (c, Z_QA * CW // GLA_DK + h)),
                  pl.BlockSpec((C, GLA_DK), lambda c, h: (c, Z_KA * CW // GLA_DK + h)),
                  pl.BlockSpec((C, GLA_DV), lambda c, h: (c, Z_VA * CW // GLA_DV + h)),
                  pl.BlockSpec((C, GLA_DK), lambda c, h: (c, h)),
                  pl.BlockSpec(sblk, lambda c, h: (c, h, 0, 0)),
                  _resident(pall.shape), _resident(masks.shape)],
        out_specs=[pl.BlockSpec((C, GLA_DV), lambda c, h: (c, h)),
                   pl.BlockSpec(sblk, lambda c, h: (c, h, 0, 0))],
        out_shape=[jax.ShapeDtypeStruct((m, va), BF16),
                   jax.ShapeDtypeStruct(s0.shape, F32)],
        compiler_params=_params(("parallel", "parallel")),
        name="gla_sample",
    )(z, z, z, la, s0, pall, masks)


def _dil_prompt_kernel(q_ref, k_ref, kp_ref, v_ref, vp_ref, o_ref, lse_ref):
    i = pl.program_id(0)
    rb = q_ref.shape[0]
    qi = lax.broadcasted_iota(jnp.int32, (DIL_STEPS, DIL_STEPS), 0)
    ki = lax.broadcasted_iota(jnp.int32, (DIL_STEPS, DIL_STEPS), 1)
    lane = lax.broadcasted_iota(jnp.int32, (DIL_STEPS, LANES), 1)
    bias_own = jnp.where(ki <= qi, 0.0, NEG)
    bias_prev = jnp.where(ki >= qi, 0.0, NEG)
    bias_first = jnp.where(i > 0, bias_prev, NEG)
    for sb in range(rb // DIL_STEPS):
        rows = slice(sb * DIL_STEPS, (sb + 1) * DIL_STEPS)
        lse_tile = jnp.zeros((DIL_STEPS, LANES), F32)
        for h in range(DIL_HEADS):
            hs = slice(h * DIL_HD, (h + 1) * DIL_HD)
            q = q_ref[rows, hs]
            if sb == 0:
                kprev, vprev, bp = kp_ref[:, hs], vp_ref[:, hs], bias_first
            else:
                prows = slice((sb - 1) * DIL_STEPS, sb * DIL_STEPS)
                kprev, vprev, bp = k_ref[prows, hs], v_ref[prows, hs], bias_prev
            s1 = _dot_nt(q, kprev) + bp
            s2 = _dot_nt(q, k_ref[rows, hs]) + bias_own
            mx = jnp.maximum(jnp.max(s1, axis=-1, keepdims=True), jnp.max(s2, axis=-1, keepdims=True))
            p1 = jnp.exp(s1 - mx)
            p2 = jnp.exp(s2 - mx)
            den = jnp.sum(p1, axis=-1, keepdims=True) + jnp.sum(p2, axis=-1, keepdims=True)
            acc = _dot(p1.astype(BF16), vprev) + _dot(p2.astype(BF16), v_ref[rows, hs])
            o_ref[rows, hs] = (acc / den).astype(BF16)
            lse_tile = jnp.where(lane == h, mx + jnp.log(den), lse_tile)
        lse_ref[rows, :] = lse_tile


def _dil_prompt(z, g, dil, rb):
    t, nz = z.shape
    l = t // dil
    zv = z.reshape(l, dil * nz)
    zb = nz // CW
    nprev = rb // DIL_STEPS
    own = lambda c: pl.BlockSpec((rb, CW), lambda i, r: (i, r * zb + c + g))
    prev = lambda c: pl.BlockSpec((DIL_STEPS, CW), lambda i, r: (jnp.maximum(i * nprev - 1, 0), r * zb + c + g))
    o, lse = pl.pallas_call(
        _dil_prompt_kernel,
        grid=(l // rb, dil),
        in_specs=[own(Z_QB), own(Z_KB), prev(Z_KB), own(Z_VB), prev(Z_VB)],
        out_specs=[pl.BlockSpec((rb, CW), lambda i, r: (i, r)),
                   pl.BlockSpec((rb, LANES), lambda i, r: (i, r))],
        out_shape=[jax.ShapeDtypeStruct((l, dil * CW), BF16),
                   jax.ShapeDtypeStruct((l, dil * LANES), F32)],
        compiler_params=_params(("parallel", "parallel")),
        name=f"dil_prompt_d{dil}",
    )(zv, zv, zv, zv, zv)
    return o.reshape(t, CW), lse.reshape(t, LANES)


def _dil_sample_kernel(*refs, dil, win, bb, tvalid, aliased):
    if aliased:
        q_ref, kvn_ref, c_ref, _, o_ref, lse_ref, co_ref = refs
    else:
        q_ref, kvn_ref, c_ref, o_ref, lse_ref, co_ref = refs
    rpt = 2 * DIL_HEADS
    keep = (win - tvalid) * rpt
    qall = q_ref[...].astype(F32)
    kvn = kvn_ref[...]
    jrow = lax.broadcasted_iota(jnp.int32, (DIL_STEPS, 1), 0)
    trow = lax.broadcasted_iota(jnp.int32, (SROWS, 1), 0)
    lane = lax.broadcasted_iota(jnp.int32, (1, LANES), 1)
    nchunk = keep // COPY_ROWS

    for b in range(bb):
        def shift(j, carry, b=b):
            dst = pl.multiple_of(j * COPY_ROWS, COPY_ROWS)
            co_ref[0, b, pl.ds(dst, COPY_ROWS), :] = c_ref[0, b, pl.ds(dst + tvalid * rpt, COPY_ROWS), :]
            return carry

        lax.fori_loop(0, nchunk, shift, 0)
        for t in range(tvalid):
            r = b * SROWS + t
            blk = jnp.concatenate([kvn[r:r + 1, j * DIL_HD:(j + 1) * DIL_HD] for j in range(rpt)], axis=0)
            co_ref[0, b, keep + t * rpt:keep + (t + 1) * rpt, :] = blk
        orows = []
        lrows = []
        knew_all = kvn[b * SROWS:(b + 1) * SROWS]
        for t in range(SROWS):
            if t >= tvalid:
                orows.append(jnp.zeros((1, DIL_OUT), F32))
                lrows.append(jnp.zeros((1, LANES), F32))
                continue
            oh = []
            lrow = jnp.zeros((1, LANES), F32)
            for h in range(DIL_HEADS):
                hs = slice(h * DIL_HD, (h + 1) * DIL_HD)
                vs = slice(DIL_OUT + h * DIL_HD, DIL_OUT + (h + 1) * DIL_HD)
                qv = qall[b * SROWS + t:b * SROWS + t + 1, hs]
                first = (t if dil > 1 else 0) * rpt + h
                kc = c_ref[0, b, pl.ds(first, DIL_STEPS, stride=rpt * dil), :]
                vc = c_ref[0, b, pl.ds(first + DIL_HEADS, DIL_STEPS, stride=rpt * dil), :]
                s_c = jnp.sum(kc * qv, axis=-1, keepdims=True)
                s_n = jnp.sum(knew_all[:, hs] * qv, axis=-1, keepdims=True)
                if dil == 1:
                    s_c = jnp.where(jrow >= t, s_c, NEG)
                    s_n = jnp.where(trow <= t, s_n, NEG)
                else:
                    s_n = jnp.where(trow == t, s_n, NEG)
                mx = jnp.maximum(jnp.max(s_c, axis=0, keepdims=True), jnp.max(s_n, axis=0, keepdims=True))
                p_c = jnp.exp(s_c - mx)
                p_n = jnp.exp(s_n - mx)
                den = jnp.sum(p_c, axis=0, keepdims=True) + jnp.sum(p_n, axis=0, keepdims=True)
                acc = (jnp.sum(p_c * vc, axis=0, keepdims=True)
                       + jnp.sum(p_n * knew_all[:, vs], axis=0, keepdims=True))
                oh.append(acc / den)
                lrow = jnp.where(lane == h, mx + jnp.log(den), lrow)
            orows.append(jnp.concatenate(oh, axis=1))
            lrows.append(lrow)
        o_ref[b * SROWS:(b + 1) * SROWS, :] = jnp.concatenate(orows, axis=0)
        lse_ref[b * SROWS:(b + 1) * SROWS, :] = jnp.concatenate(lrows, axis=0)


def _dil_sample(qf, kvn, cache, cache_out, layer, g, tvalid, bb):
    win, dil = DIL_PAIRS[g]
    depth, nb, wr, _ = cache.shape
    m = qf.shape[0]
    rows = bb * SROWS
    cblk = (1, bb, wr, LANES)
    aliased = cache_out is not None
    in_specs = [pl.BlockSpec((rows, CW), lambda i: (i, g)),
                pl.BlockSpec((rows, 2 * CW), lambda i: (i, g)),
                pl.BlockSpec(cblk, lambda i: (layer, i, 0, 0))]
    args = [qf, kvn, cache]
    if aliased:
        in_specs.append(pl.BlockSpec(memory_space=pl.ANY))
        args.append(cache_out)
    return pl.pallas_call(
        functools.partial(_dil_sample_kernel, dil=dil, win=win, bb=bb, tvalid=tvalid, aliased=aliased),
        grid=(nb // bb,),
        in_specs=in_specs,
        out_specs=[pl.BlockSpec((rows, CW), lambda i: (i, 0)),
                   pl.BlockSpec((rows, LANES), lambda i: (i, 0)),
                   pl.BlockSpec(cblk, lambda i: (layer, i, 0, 0))],
        out_shape=[jax.ShapeDtypeStruct((m, CW), F32),
                   jax.ShapeDtypeStruct((m, LANES), F32),
                   jax.ShapeDtypeStruct(cache.shape, F32)],
        input_output_aliases={3: 2} if aliased else {},
        compiler_params=_params(("parallel",)),
        name=f"dil_sample_d{dil}",
    )(*args)


def _merge_kernel(x_ref, oa_ref, r_ref, ga_ref, gb_ref, o0_ref, o1_ref, o2_ref, l0_ref, l1_ref, l2_ref,
                  gn_ref, wa_ref, wb_ref, wo_ref, nw_ref, h_ref):
    oa = oa_ref[...].astype(F32)
    gn = gn_ref[...]
    parts = [_rms(oa[:, h * GLA_DV:(h + 1) * GLA_DV], gn) for h in range(GLA_HEADS)]
    r = r_ref[...].astype(F32)
    oan = (jnp.concatenate(parts, axis=1) * (r * jax.nn.sigmoid(r))).astype(BF16)

    lses = [l0_ref[...], l1_ref[...], l2_ref[...]]
    mx = jnp.maximum(jnp.maximum(lses[0], lses[1]), lses[2])
    es = [jnp.exp(l - mx) for l in lses]
    inv = 1.0 / (es[0] + es[1] + es[2])
    outs = [o0_ref[...].astype(F32), o1_ref[...].astype(F32), o2_ref[...].astype(F32)]
    ob = []
    for h in range(DIL_HEADS):
        hs = slice(h * DIL_HD, (h + 1) * DIL_HD)
        acc = None
        for e, o in zip(es, outs):
            term = (e * inv)[:, h:h + 1] * o[:, hs]
            acc = term if acc is None else acc + term
        ob.append(acc)
    ob = jnp.concatenate(ob, axis=1).astype(BF16)

    merged = (ga_ref[...].astype(F32) * _dot(oan, wa_ref[...])
              + gb_ref[...].astype(F32) * _dot(ob, wb_ref[...]))
    y = _dot(merged.astype(BF16), wo_ref[...])
    h_ref[...] = x_ref[...] + _rms(y, nw_ref[...])


def _merge(x, oa, z, dil_o, dil_lse, gn, wa, wb, wo, nw, tm):
    m, d = x.shape
    va = oa.shape[1]
    zc = lambda c: pl.BlockSpec((tm, d), lambda i: (i, c * CW // d))
    row = lambda w: pl.BlockSpec((tm, w), lambda i: (i, 0))
    return pl.pallas_call(
        _merge_kernel,
        grid=(m // tm,),
        in_specs=[row(d), row(va), zc(Z_RA), zc(Z_GA), zc(Z_GB),
                  row(CW), row(CW), row(CW), row(LANES), row(LANES), row(LANES),
                  _resident(gn.shape), _resident(wa.shape), _resident(wb.shape),
                  _resident(wo.shape), _resident(nw.shape)],
        out_specs=row(d),
        out_shape=jax.ShapeDtypeStruct((m, d), F32),
        compiler_params=_params(("parallel",)),
        name="merge",
    )(x, oa, z, z, z, *dil_o, *dil_lse, gn, wa, wb, wo, nw)


def _gelu_tanh(x):
    return 0.5 * x * (1.0 + jnp.tanh(0.7978845608028654 * (x + 0.044715 * x * x * x)))


def _ffn_kernel(*refs, fc, sample):
    if sample:
        h_ref, nw1_ref, wup_ref, wc_ref, bc_ref, wdn_ref, nw2_ref, p1_ref, p2_ref, out_ref, gst_ref, gbuf = refs
    else:
        h_ref, nw1_ref, wup_ref, wc_ref, bc_ref, wdn_ref, nw2_ref, out_ref, gst_ref, gbuf = refs
    tm = h_ref.shape[0]
    dff = wdn_ref.shape[0]
    i = pl.program_id(0)

    @pl.when(i == 0)
    def _():
        gbuf[0:SUBLANES, :] = jnp.zeros((SUBLANES, dff), F32)

    h = h_ref[...]
    hn = _rms(h, nw1_ref[...]).astype(BF16)
    if sample:
        tpos = lax.broadcasted_iota(jnp.int32, (tm, 1), 0) % SROWS
    acc = None
    for c in range(dff // fc):
        cs = slice(c * fc, (c + 1) * fc)
        gp = _dot(hn, wup_ref[:, cs])
        val = _dot(hn, wup_ref[:, dff + c * fc:dff + (c + 1) * fc])
        gbuf[SUBLANES:SUBLANES + tm, cs] = gp
        g1 = gbuf[SUBLANES - 1:SUBLANES - 1 + tm, cs]
        g2 = gbuf[SUBLANES - 2:SUBLANES - 2 + tm, cs]
        if sample:
            g1 = jnp.where(tpos == 0, p1_ref[:, cs], g1)
            g2 = jnp.where(tpos < 2, p2_ref[:, cs], g2)
        conv = g2 * wc_ref[0:1, cs] + g1 * wc_ref[1:2, cs] + gp * wc_ref[2:3, cs] + bc_ref[:, cs]
        act = (_gelu_tanh(conv) * val).astype(BF16)
        part = _dot(act, wdn_ref[cs, :])
        acc = part if acc is None else acc + part
        if sample:
            gst_ref[:, cs] = gp
        else:
            gbuf[0:SUBLANES, cs] = gbuf[tm:tm + SUBLANES, cs]
    if not sample:
        gst_ref[...] = gbuf[0:SUBLANES, :]
    out_ref[...] = h + _rms(acc, nw2_ref[...])


def _ffn(h, nw1, wup, wc, bc, wdn, nw2, prev, tm, fc):
    m, d = h.shape
    dff = wdn.shape[0]
    sample = prev is not None
    row = lambda w: pl.BlockSpec((tm, w), lambda i: (i, 0))
    in_specs = [row(d), _resident((1, d)), _resident(wup.shape), _resident(wc.shape),
                _resident((1, dff)), _resident(wdn.shape), _resident((1, d))]
    args = [h, nw1, wup, wc, bc, wdn, nw2]
    if sample:
        in_specs += [row(dff), row(dff)]
        args += list(prev)
        gst_spec, gst_shape = row(dff), (m, dff)
    else:
        gst_spec = pl.BlockSpec((SUBLANES, dff), lambda i: (i, 0))
        gst_shape = (m // tm * SUBLANES, dff)
    return pl.pallas_call(
        functools.partial(_ffn_kernel, fc=fc, sample=sample),
        grid=(m // tm,),
        in_specs=in_specs,
        out_specs=[row(d), gst_spec],
        out_shape=[jax.ShapeDtypeStruct((m, d), F32), jax.ShapeDtypeStruct(gst_shape, F32)],
        scratch_shapes=[pltpu.VMEM((tm + 2 * SUBLANES, dff), F32)],
        compiler_params=_params(("arbitrary",)),
        name="ffn_sample" if sample else "ffn_prompt",
    )(*args)


def _rope_tables(pos):
    inv = ROPE_THETA ** (-jnp.arange(ROPE_HALF, dtype=F32) * 2.0 / ROPE_DIM)
    ang = pos[:, None] * inv[None, :]
    cos, sin = jnp.cos(ang), jnp.sin(ang)
    pad = LANES - ROPE_DIM
    one = jnp.ones((pos.shape[0], pad), F32)
    cos_t = jnp.concatenate([cos, cos, one], axis=1)
    sin_t = jnp.concatenate([-sin, sin, 0.0 * one], axis=1)
    return cos_t, sin_t


def _prep_in(w_in_l, d):
    qk = GLA_HEADS * GLA_DK
    va = GLA_HEADS * GLA_DV
    nb = N_GROUPS * DIL_OUT
    sizes = (qk, qk, va, va, GLA_RANK, 3 * nb, 2 * d)
    offs = np.cumsum((0,) + sizes)
    q_a, k_a, v_a, r_a, a_low, qkv_b, gate = (w_in_l[:, offs[j]:offs[j + 1]] for j in range(7))
    q_b, k_b, v_b = qkv_b[:, :nb], qkv_b[:, nb:2 * nb], qkv_b[:, 2 * nb:]
    wz = jnp.concatenate([q_a, k_a, v_a, r_a, gate, q_b, k_b, v_b], axis=1).astype(BF16)
    wal = jnp.pad(a_low, ((0, 0), (0, LANES - GLA_RANK))).astype(BF16)
    kv = []
    for g in range(N_GROUPS):
        kv += [k_b[:, g * DIL_OUT:(g + 1) * DIL_OUT], v_b[:, g * DIL_OUT:(g + 1) * DIL_OUT]]
    wkv = jnp.concatenate(kv, axis=1).astype(BF16)
    return wz, wal, wkv


def kernel(x_prompt, x_sample, state_gla, cache_dil_w128, cache_dil_w512, cache_dil_w2048, state_ffn_conv,
           w_in, w_gate_up, b_gate, gla_norm, w_branch_a, w_branch_b, w_out, norm_pre_mix, norm_post_mix,
           norm_pre_ffn, norm_post_ffn, w_ffn_up, w_conv, b_conv, w_ffn_down):
    depth, d = w_in.shape[0], w_in.shape[1]
    _, t_p, _ = x_prompt.shape
    nb, t_s, _ = x_sample.shape
    dff = w_ffn_down.shape[1]
    caches = (cache_dil_w128, cache_dil_w512, cache_dil_w2048)
    assert x_prompt.shape[0] == 1 and t_s <= SROWS // 2 and d == 2 * DIL_OUT
    for c, (win, _) in zip(caches, DIL_PAIRS):
        assert c.shape[2] == win and t_p >= win
    tail = max(w for w, _ in DIL_PAIRS)
    tm = 256
    m_s = nb * SROWS
    tm_s = min(tm, m_s)
    tm_ffn_s = min(128, m_s)

    xp = x_prompt.reshape(t_p, d)
    xs = jnp.pad(x_sample, ((0, 0), (0, SROWS - t_s), (0, 0))).reshape(nb * SROWS, d)
    cos_p, sin_p = _rope_tables(jnp.arange(t_p, dtype=F32))
    pos_s = PAST_LEN + jnp.minimum(jnp.arange(SROWS), t_s - 1).astype(F32)
    cos_s, sin_s = (jnp.tile(a, (nb, 1)) for a in _rope_tables(pos_s))
    cache_v = [c.reshape(depth, nb, c.shape[2] * 2 * DIL_HEADS, DIL_HD) for c in caches]
    new_cache = [None] * N_GROUPS
    bbs = (8, 4, 1)

    p_gla, p_kv, p_conv, s_gla, s_conv = [], [], [], [], []
    row2 = lambda a: a.reshape(1, -1)
    for i in range(depth):
        wz, wal, wkv = _prep_in(w_in[i], d)
        wgu = jnp.pad(w_gate_up[i], ((0, LANES - GLA_RANK), (0, 0))).astype(BF16)
        bg = row2(b_gate[i])
        wa, wb, wo = (w.astype(BF16) for w in (w_branch_a[i], w_branch_b[i], w_out[i]))
        wup, wdn = w_ffn_up[i].astype(BF16), w_ffn_down[i].astype(BF16)
        n1, n2, n3, n4 = (row2(n[i]) for n in (norm_pre_mix, norm_post_mix, norm_pre_ffn, norm_post_ffn))
        gn = row2(gla_norm[i])
        wc = jnp.pad(w_conv[i], ((0, SUBLANES - CONV_W), (0, 0)))
        bc = row2(b_conv[i])

        z, la = _inproj(xp, n1, wz, wal, wgu, bg, cos_p, sin_p, tm)
        kvt = _kvtail(xp[t_p - tail:], n1, wkv, cos_p[t_p - tail:], sin_p[t_p - tail:], tm)
        oa, sfin = _gla_prompt(z, la)
        dil = [_dil_prompt(z, g, dl, min(1024, t_p // dl)) for g, (_, dl) in enumerate(DIL_PAIRS)]
        h = _merge(xp, oa, z, [o for o, _ in dil], [l for _, l in dil], gn, wa, wb, wo, n2, tm)
        xp, gst = _ffn(h, n3, wup, wc, bc, wdn, n4, None, tm, 512)
        p_gla.append(sfin[None])
        p_kv.append(kvt)
        p_conv.append(gst[-(CONV_W - 1):][None])

        z, la = _inproj(xs, n1, wz, wal, wgu, bg, cos_s, sin_s, tm_s)
        kvn = _kvtail(xs, n1, wkv, cos_s, sin_s, tm_s)
        oa, sfin = _gla_sample(z, la, state_gla[i], t_s)
        qf = z[:, Z_QB * CW:Z_KB * CW].astype(F32)
        dil_o, dil_l = [], []
        for g in range(N_GROUPS):
            o, l, new_cache[g] = _dil_sample(qf, kvn, cache_v[g], new_cache[g], i, g, t_s, min(bbs[g], nb))
            dil_o.append(o)
            dil_l.append(l)
        h = _merge(xs, oa, z, dil_o, dil_l, gn, wa, wb, wo, n2, tm_s)
        st = state_ffn_conv[i]
        zpad = lambda a: jnp.pad(a, ((0, 0), (0, SROWS - a.shape[1]), (0, 0))).reshape(m_s, dff)
        prev = (zpad(st[:, 1:2]), zpad(st))
        xs, gpre = _ffn(h, n3, wup, wc, bc, wdn, n4, prev, tm_ffn_s, 512)
        s_gla.append(sfin)
        s_conv.append(gpre.reshape(nb, SROWS, dff)[:, t_s - (CONV_W - 1):t_s])

    y_p = xp.reshape(1, t_p, d)
    y_s = xs.reshape(nb, SROWS, d)[:, :t_s]
    p_caches = []
    for g, (win, _) in enumerate(DIL_PAIRS):
        rows = [kv[tail - win:, g * 2 * CW:(g + 1) * 2 * CW].reshape(1, win, 2, DIL_HEADS, DIL_HD) for kv in p_kv]
        p_caches.append(jnp.stack(rows))
    s_caches = [c.reshape(o.shape) for c, o in zip(new_cache, caches)]
    return (y_p, y_s, jnp.stack(p_gla), *p_caches, jnp.stack(p_conv),
            jnp.stack(s_gla), *s_caches, jnp.stack(s_conv))
```

```python
import functools

import numpy as np
import jax
import jax.numpy as jnp
from jax import lax
from jax.experimental import pallas as pl
from jax.experimental.pallas import tpu as pltpu

F32 = jnp.float32
BF16 = jnp.bfloat16

GLA_HEADS = 4
GLA_DK = 128
GLA_DV = 256
GLA_RANK = 16
GLA_TAU = 16.0
DIL_PAIRS = ((128, 1), (512, 4), (2048, 16))
N_GROUPS = len(DIL_PAIRS)
DIL_HEADS = 4
DIL_HD = 128
DIL_STEPS = 128
DIL_OUT = DIL_HEADS * DIL_HD
ROPE_THETA = 500000.0
ROPE_DIM = DIL_HD // 4
ROPE_HALF = ROPE_DIM // 2
CONV_W = 3
EPS = 1e-6
PAST_LEN = 16384

LANES = 128
SUBLANES = 8
VMEM_LIMIT = 56 * 1024 * 1024

CW = 512
Z_QA, Z_KA, Z_VA, Z_RA, Z_GA, Z_GB, Z_QB, Z_KB, Z_VB = 0, 1, 2, 4, 6, 8, 10, 13, 16
Z_BLOCKS = 19
SROWS = 8
GLA_C = 128
NEG = -1e30
COPY_ROWS = 32


def _params(sem):
    return pltpu.CompilerParams(dimension_semantics=sem, vmem_limit_bytes=VMEM_LIMIT)


def _resident(shape):
    nd = len(shape)
    return pl.BlockSpec(shape, lambda *_: (0,) * nd, pipeline_mode=pl.Buffered(1))


def _rms(x, w):
    ms = jnp.mean(x * x, axis=-1, keepdims=True)
    return x * lax.rsqrt(ms + EPS) * w


def _dot(a, b):
    return jnp.dot(a, b, preferred_element_type=F32)


def _dot_nt(a, b):
    return lax.dot_general(a, b, (((1,), (1,)), ((), ())), preferred_element_type=F32)


def _rope_heads(acc, cos_t, sin_t, lt):
    parts = []
    for h in range(DIL_HEADS):
        a = acc[:, h * LANES:(h + 1) * LANES]
        up = pltpu.roll(a, LANES - ROPE_HALF, axis=1)
        dn = pltpu.roll(a, ROPE_HALF, axis=1)
        parts.append(a * cos_t + jnp.where(lt, up, dn) * sin_t)
    return parts


def _inproj_kernel(x_ref, nw_ref, wz_ref, wal_ref, wgu_ref, bg_ref, cos_ref, sin_ref,
                   z_ref, la_ref, g0_ref, g1_ref, g2_ref, scr, *, dils):
    tm = x_ref.shape[0]
    xn = _rms(x_ref[...], nw_ref[...]).astype(BF16)
    cos_t = cos_ref[...]
    sin_t = sin_ref[...]
    lt = lax.broadcasted_iota(jnp.int32, cos_t.shape, 1) < ROPE_HALF
    grefs = (g0_ref, g1_ref, g2_ref)
    for c in range(Z_BLOCKS):
        acc = _dot(xn, wz_ref[:, c * CW:(c + 1) * CW])
        if c < Z_QB:
            if c == Z_QA:
                acc = acc * (GLA_DK ** -0.5)
            elif c >= Z_GA:
                acc = jax.nn.sigmoid(acc)
            z_ref[:, c * CW:(c + 1) * CW] = acc.astype(BF16)
            continue
        part, g = divmod(c - Z_QB, N_GROUPS)
        if part < 2:
            acc = jnp.concatenate(_rope_heads(acc, cos_t, sin_t, lt), axis=1)
        if part == 0:
            acc = acc * (DIL_HD ** -0.5)
        cols = slice(part * CW, (part + 1) * CW)
        dil = dils[g]
        if dil == 1:
            grefs[g][0, :, cols] = acc.astype(BF16)
        else:
            for j in range(CW // LANES):
                scr[j] = acc[:, j * LANES:(j + 1) * LANES]
            for r in range(dil):
                rows = [scr[j, pl.ds(r, tm // dil, stride=dil), :] for j in range(CW // LANES)]
                grefs[g][r, :, cols] = jnp.concatenate(rows, axis=1).astype(BF16)
    al = _dot(xn, wal_ref[...]).astype(BF16)
    lg = _dot(al, wgu_ref[...]) + bg_ref[...]
    la_ref[...] = (jnp.minimum(lg, 0.0) - jnp.log1p(jnp.exp(-jnp.abs(lg)))) * (1.0 / GLA_TAU)


def _inproj(x, nw, wz, wal, wgu, bg, cos_t, sin_t, tm, dils):
    m, d = x.shape
    nw_all = wz.shape[1]
    nz = Z_QB * CW
    qk = wgu.shape[1]
    row = lambda w: pl.BlockSpec((tm, w), lambda i: (i, 0))
    gspec = lambda dl: pl.BlockSpec((dl, tm // dl, 3 * CW), lambda i: (0, i, 0))
    return pl.pallas_call(
        functools.partial(_inproj_kernel, dils=dils),
        grid=(m // tm,),
        in_specs=[row(d), _resident((1, d)), _resident((d, nw_all)), _resident((d, LANES)),
                  _resident((LANES, qk)), _resident((1, qk)), row(LANES), row(LANES)],
        out_specs=[row(nz), row(qk)] + [gspec(dl) for dl in dils],
        out_shape=[jax.ShapeDtypeStruct((m, nz), BF16), jax.ShapeDtypeStruct((m, qk), F32)]
        + [jax.ShapeDtypeStruct((dl, m // dl, 3 * CW), BF16) for dl in dils],
        scratch_shapes=[pltpu.VMEM((CW // LANES, tm, LANES), F32)],
        compiler_params=_params(("parallel",)),
        name="inproj",
    )(x, nw, wz, wal, wgu, bg, cos_t, sin_t)


def _kvtail_kernel(x_ref, nw_ref, w_ref, cos_ref, sin_ref, o_ref):
    xn = _rms(x_ref[...], nw_ref[...]).astype(BF16)
    cos_t = cos_ref[...]
    sin_t = sin_ref[...]
    lt = lax.broadcasted_iota(jnp.int32, cos_t.shape, 1) < ROPE_HALF
    for c in range(2 * N_GROUPS):
        acc = _dot(xn, w_ref[:, c * CW:(c + 1) * CW])
        if c % 2 == 0:
            for h, p in enumerate(_rope_heads(acc, cos_t, sin_t, lt)):
                o_ref[:, c * CW + h * LANES:c * CW + (h + 1) * LANES] = p
        else:
            o_ref[:, c * CW:(c + 1) * CW] = acc


def _kvtail(x, nw, wkv, cos_t, sin_t, tm):
    m, d = x.shape
    n = wkv.shape[1]
    row = lambda w: pl.BlockSpec((tm, w), lambda i: (i, 0))
    return pl.pallas_call(
        _kvtail_kernel,
        grid=(m // tm,),
        in_specs=[row(d), _resident((1, d)), _resident((d, n)), row(LANES), row(LANES)],
        out_specs=row(n),
        out_shape=jax.ShapeDtypeStruct((m, n), F32),
        compiler_params=_params(("parallel",)),
        name="kvtail",
    )(x, nw, wkv, cos_t, sin_t)


@functools.lru_cache(maxsize=None)
def _gla_consts(C, nl, seg):
    x = np.arange(C)[:, None]
    i = np.arange(C)[None, :]
    same = (x // seg == i // seg) if seg else np.ones((C, C), bool)
    mats = [(i <= x) & same, (i > x) & same]
    masks = [x == i]
    for l in range(nl):
        m = 1 << l
        r = (x | (2 * m - 1)) - m
        mats.append((i > np.minimum(x, r)) & (i <= np.maximum(x, r)))
        masks.append((((x >> l) ^ (i >> l)) == 1) & (((x >> l) & 1) == 1))
    pall = np.concatenate(mats, 0).astype(np.float32)
    return pall, np.stack(masks).astype(np.float32)


def _gla_core(q, k, v, g, pall_ref, mask_ref, nl):
    C = q.shape[0]
    g_hi = g.astype(BF16)
    g_lo = (g - g_hi.astype(F32)).astype(BF16)
    ex = _dot(pall_ref[...], jnp.concatenate([g_hi, g_lo], axis=1))
    ex = ex[:, :GLA_DK] + ex[:, GLA_DK:]
    b = ex[0:C]
    aft = ex[C:2 * C]
    qf = q.astype(F32)
    kf = k.astype(F32)
    a = _dot_nt(q, k) * mask_ref[0]
    for l in range(nl):
        e = jnp.exp(ex[(2 + l) * C:(3 + l) * C])
        a = a + _dot_nt((qf * e).astype(BF16), (kf * e).astype(BF16)) * mask_ref[1 + l]
    o = _dot(a.astype(BF16), v)
    qe = (qf * jnp.exp(b)).astype(BF16)
    kdt = (kf * jnp.exp(aft)).T.astype(BF16)
    return o, qe, kdt, b.T


def _gla_prompt_kernel(q_ref, k_ref, v_ref, g_ref, pall_ref, mask_ref, o_ref, sf_ref, s_scr, *, nl):
    i = pl.program_id(0)

    @pl.when(i == 0)
    def _():
        s_scr[...] = jnp.zeros_like(s_scr)

    C = q_ref.shape[0]
    for h in range(GLA_HEADS):
        ks = slice(h * GLA_DK, (h + 1) * GLA_DK)
        vs = slice(h * GLA_DV, (h + 1) * GLA_DV)
        v = v_ref[:, vs]
        o, qe, kdt, bt = _gla_core(q_ref[:, ks], k_ref[:, ks], v, g_ref[:, ks], pall_ref, mask_ref, nl)
        s = s_scr[h]
        o_ref[:, vs] = (o + _dot(qe, s.astype(BF16))).astype(BF16)
        s_scr[h] = s * jnp.exp(bt[:, C - 1:C]) + _dot(kdt, v)

    @pl.when(i == pl.num_programs(0) - 1)
    def _():
        sf_ref[...] = s_scr[...]


def _gla_prompt(z, la):
    m = z.shape[0]
    C = GLA_C
    nl = C.bit_length() - 1
    pall, masks = _gla_consts(C, nl, 0)
    pall = jnp.asarray(pall, BF16)
    masks = jnp.asarray(masks, F32)
    qk = GLA_HEADS * GLA_DK
    va = GLA_HEADS * GLA_DV
    return pl.pallas_call(
        functools.partial(_gla_prompt_kernel, nl=nl),
        grid=(m // C,),
        in_specs=[pl.BlockSpec((C, qk), lambda i: (i, Z_QA)),
                  pl.BlockSpec((C, qk), lambda i: (i, Z_KA)),
                  pl.BlockSpec((C, va), lambda i: (i, Z_VA * CW // va)),
                  pl.BlockSpec((C, qk), lambda i: (i, 0)),
                  _resident(pall.shape), _resident(masks.shape)],
        out_specs=[pl.BlockSpec((C, va), lambda i: (i, 0)),
                   pl.BlockSpec((GLA_HEADS, GLA_DK, GLA_DV), lambda i: (0, 0, 0))],
        out_shape=[jax.ShapeDtypeStruct((m, va), BF16),
                   jax.ShapeDtypeStruct((GLA_HEADS, GLA_DK, GLA_DV), F32)],
        scratch_shapes=[pltpu.VMEM((GLA_HEADS, GLA_DK, GLA_DV), F32)],
        compiler_params=_params(("arbitrary",)),
        name="gla_prompt",
    )(z, z, z, la, pall, masks)


def _gla_sample_kernel(q_ref, k_ref, v_ref, g_ref, s0_ref, pall_ref, mask_ref, o_ref, sf_ref, *, nl, tvalid):
    C = q_ref.shape[0]
    nseq = C // SROWS
    row = lax.broadcasted_iota(jnp.int32, (C, 1), 0)
    g = jnp.where(row % SROWS < tvalid, g_ref[...], 0.0)
    v = v_ref[...]
    o, qe, kdt, bt = _gla_core(q_ref[...], k_ref[...], v, g, pall_ref, mask_ref, nl)
    zero_v = jnp.zeros_like(v)
    for j in range(nseq):
        mine = row // SROWS == j
        s = s0_ref[j, 0]
        o = o + jnp.where(mine, _dot(qe, s.astype(BF16)), 0.0)
        last = j * SROWS + SROWS - 1
        sf_ref[j, 0] = s * jnp.exp(bt[:, last:last + 1]) + _dot(kdt, jnp.where(mine, v, zero_v))
    o_ref[...] = o.astype(BF16)


def _gla_sample(z, la, s0, tvalid):
    m = z.shape[0]
    C = GLA_C
    nseq = C // SROWS
    nl = (tvalid - 1).bit_length()
    pall, masks = _gla_consts(C, nl, SROWS)
    pall = jnp.asarray(pall, BF16)
    masks = jnp.asarray(masks, F32)
    va = GLA_HEADS * GLA_DV
    sblk = (nseq, 1, GLA_DK, GLA_DV)
    return pl.pallas_call(
        functools.partial(_gla_sample_kernel, nl=nl, tvalid=tvalid),
        grid=(m // C, GLA_HEADS),
        in_specs=[pl.BlockSpec((C, GLA_DK), lambda c, h: (c, Z_QA * CW // GLA_DK + h)),
                  pl.BlockSpec((C, GLA_DK), lambda c, h: (c, Z_KA * CW // GLA_DK + h)),
                  pl.BlockSpec((C, GLA_DV), lambda c, h: (c, Z_VA * CW // GLA_DV + h)),
                  pl.BlockSpec((C, GLA_DK), lambda c, h: (c, h)),
                  pl.BlockSpec(sblk, lambda c, h: (c, h, 0, 0)),
                  _resident(pall.shape), _resident(masks.shape)],
        out_specs=[pl.BlockSpec((C, GLA_DV), lambda c, h: (c, h)),
                   pl.BlockSpec(sblk, lambda c, h: (c, h, 0, 0))],
        out_shape=[jax.ShapeDtypeStruct((m, va), BF16),
                   jax.ShapeDtypeStruct(s0.shape, F32)],
        compiler_params=_params(("parallel", "parallel")),
        name="gla_sample",
    )(z, z, z, la, s0, pall, masks)


def _dil_prompt_kernel(q_ref, k_ref, kp_ref, v_ref, vp_ref, o_ref, lse_ref):
    i = pl.program_id(1)
    rb = q_ref.shape[0]
    nk = 2 * DIL_STEPS
    qi = lax.broadcasted_iota(jnp.int32, (DIL_STEPS, nk), 0)
    ki = lax.broadcasted_iota(jnp.int32, (DIL_STEPS, nk), 1)
    lane = lax.broadcasted_iota(jnp.int32, (DIL_STEPS, LANES), 1)
    band = (ki >= qi) & (ki <= qi + DIL_STEPS)
    bias = jnp.where(band, 0.0, NEG)
    bias_first = jnp.where((i > 0) | (ki >= DIL_STEPS), bias, NEG)
    ones = jnp.ones((nk, DIL_HD), BF16)
    for sb in range(rb // DIL_STEPS):
        rows = slice(sb * DIL_STEPS, (sb + 1) * DIL_STEPS)
        lse_tile = jnp.zeros((DIL_STEPS, LANES), F32)
        for h in range(DIL_HEADS):
            hs = slice(h * DIL_HD, (h + 1) * DIL_HD)
            if sb == 0:
                kcat = jnp.concatenate([kp_ref[:, hs], k_ref[rows, hs]], axis=0)
                vcat = jnp.concatenate([vp_ref[:, hs], v_ref[rows, hs]], axis=0)
            else:
                krows = slice((sb - 1) * DIL_STEPS, (sb + 1) * DIL_STEPS)
                kcat, vcat = k_ref[krows, hs], v_ref[krows, hs]
            s = _dot_nt(q_ref[rows, hs], kcat) + (bias_first if sb == 0 else bias)
            mx = jnp.max(s, axis=-1, keepdims=True)
            p = jnp.exp(s - mx).astype(BF16)
            pv = _dot(p, jnp.concatenate([vcat, ones], axis=1))
            den = pv[:, DIL_HD:]
            o_ref[rows, hs] = (pv[:, :DIL_HD] / den).astype(BF16)
            lse_tile = jnp.where(lane == h, mx + jnp.log(den), lse_tile)
        lse_ref[rows, :] = lse_tile


def _dil_prompt(qkv, rb):
    dil, l, _ = qkv.shape
    nprev = rb // DIL_STEPS
    own = lambda c: pl.BlockSpec((None, rb, CW), lambda r, i: (r, i, c))
    prev = lambda c: pl.BlockSpec((None, DIL_STEPS, CW), lambda r, i: (r, jnp.maximum(i * nprev - 1, 0), c))
    return pl.pallas_call(
        _dil_prompt_kernel,
        grid=(dil, l // rb),
        in_specs=[own(0), own(1), prev(1), own(2), prev(2)],
        out_specs=[pl.BlockSpec((None, rb, CW), lambda r, i: (r, i, 0)),
                   pl.BlockSpec((None, rb, LANES), lambda r, i: (r, i, 0))],
        out_shape=[jax.ShapeDtypeStruct((dil, l, CW), BF16),
                   jax.ShapeDtypeStruct((dil, l, LANES), F32)],
        compiler_params=_params(("parallel", "parallel")),
        name=f"dil_prompt_d{dil}",
    )(qkv, qkv, qkv, qkv, qkv)


def _dil_sample_kernel(*refs, dil, win, bb, tvalid, aliased):
    if aliased:
        q_ref, kvn_ref, c_ref, _, o_ref, lse_ref, co_ref = refs
    else:
        q_ref, kvn_ref, c_ref, o_ref, lse_ref, co_ref = refs
    rpt = 2 * DIL_HEADS
    keep = (win - tvalid) * rpt
    qall = q_ref[...].astype(F32)
    kvn = kvn_ref[...]
    jrow = lax.broadcasted_iota(jnp.int32, (DIL_STEPS, 1), 0)
    trow = lax.broadcasted_iota(jnp.int32, (SROWS, 1), 0)
    lane = lax.broadcasted_iota(jnp.int32, (1, LANES), 1)
    nchunk = keep // COPY_ROWS

    for b in range(bb):
        def shift(j, carry, b=b):
            dst = pl.multiple_of(j * COPY_ROWS, COPY_ROWS)
            co_ref[0, b, pl.ds(dst, COPY_ROWS), :] = c_ref[0, b, pl.ds(dst + tvalid * rpt, COPY_ROWS), :]
            return carry

        lax.fori_loop(0, nchunk, shift, 0)
        for t in range(tvalid):
            r = b * SROWS + t
            blk = jnp.concatenate([kvn[r:r + 1, j * DIL_HD:(j + 1) * DIL_HD] for j in range(rpt)], axis=0)
            co_ref[0, b, keep + t * rpt:keep + (t + 1) * rpt, :] = blk
        orows = []
        lrows = []
        knew_all = kvn[b * SROWS:(b + 1) * SROWS]
        for t in range(SROWS):
            if t >= tvalid:
                orows.append(jnp.zeros((1, DIL_OUT), F32))
                lrows.append(jnp.zeros((1, LANES), F32))
                continue
            oh = []
            lrow = jnp.zeros((1, LANES), F32)
            for h in range(DIL_HEADS):
                hs = slice(h * DIL_HD, (h + 1) * DIL_HD)
                vs = slice(DIL_OUT + h * DIL_HD, DIL_OUT + (h + 1) * DIL_HD)
                qv = qall[b * SROWS + t:b * SROWS + t + 1, hs]
                first = (t if dil > 1 else 0) * rpt + h
                kc = c_ref[0, b, pl.ds(first, DIL_STEPS, stride=rpt * dil), :]
                vc = c_ref[0, b, pl.ds(first + DIL_HEADS, DIL_STEPS, stride=rpt * dil), :]
                s_c = jnp.sum(kc * qv, axis=-1, keepdims=True)
                s_n = jnp.sum(knew_all[:, hs] * qv, axis=-1, keepdims=True)
                if dil == 1:
                    s_c = jnp.where(jrow >= t, s_c, NEG)
                    s_n = jnp.where(trow <= t, s_n, NEG)
                else:
                    s_n = jnp.where(trow == t, s_n, NEG)
                mx = jnp.maximum(jnp.max(s_c, axis=0, keepdims=True), jnp.max(s_n, axis=0, keepdims=True))
                p_c = jnp.exp(s_c - mx)
                p_n = jnp.exp(s_n - mx)
                den = jnp.sum(p_c, axis=0, keepdims=True) + jnp.sum(p_n, axis=0, keepdims=True)
                acc = (jnp.sum(p_c * vc, axis=0, keepdims=True)
                       + jnp.sum(p_n * knew_all[:, vs], axis=0, keepdims=True))
                oh.append(acc / den)
                lrow = jnp.where(lane == h, mx + jnp.log(den), lrow)
            orows.append(jnp.concatenate(oh, axis=1))
            lrows.append(lrow)
        o_ref[b * SROWS:(b + 1) * SROWS, :] = jnp.concatenate(orows, axis=0)
        lse_ref[b * SROWS:(b + 1) * SROWS, :] = jnp.concatenate(lrows, axis=0)


def _dil_sample(qf, kvn, cache, cache_out, layer, g, tvalid, bb):
    win, dil = DIL_PAIRS[g]
    depth, nb, wr, _ = cache.shape
    m = qf.shape[0]
    rows = bb * SROWS
    cblk = (1, bb, wr, LANES)
    aliased = cache_out is not None
    in_specs = [pl.BlockSpec((rows, CW), lambda i: (i, 0)),
                pl.BlockSpec((rows, 2 * CW), lambda i: (i, g)),
                pl.BlockSpec(cblk, lambda i: (layer, i, 0, 0))]
    args = [qf, kvn, cache]
    if aliased:
        in_specs.append(pl.BlockSpec(memory_space=pl.ANY))
        args.append(cache_out)
    return pl.pallas_call(
        functools.partial(_dil_sample_kernel, dil=dil, win=win, bb=bb, tvalid=tvalid, aliased=aliased),
        grid=(nb // bb,),
        in_specs=in_specs,
        out_specs=[pl.BlockSpec((rows, CW), lambda i: (i, 0)),
                   pl.BlockSpec((rows, LANES), lambda i: (i, 0)),
                   pl.BlockSpec(cblk, lambda i: (layer, i, 0, 0))],
        out_shape=[jax.ShapeDtypeStruct((m, CW), F32),
                   jax.ShapeDtypeStruct((m, LANES), F32),
                   jax.ShapeDtypeStruct(cache.shape, F32)],
        input_output_aliases={3: 2} if aliased else {},
        compiler_params=_params(("parallel",)),
        name=f"dil_sample_d{dil}",
    )(*args)


def _tokens_from_streams(ref, scr):
    dil, n, w = ref.shape
    if dil == 1:
        return ref[0].astype(F32)
    nslab = w // LANES
    for r in range(dil):
        blk = ref[r].astype(F32)
        for j in range(nslab):
            scr[j, pl.ds(r, n, stride=dil), :] = blk[:, j * LANES:(j + 1) * LANES]
    return jnp.concatenate([scr[j] for j in range(nslab)], axis=1)


def _merge_kernel(x_ref, oa_ref, r_ref, ga_ref, gb_ref, o0_ref, o1_ref, o2_ref, l0_ref, l1_ref, l2_ref,
                  gn_ref, wa_ref, wb_ref, wo_ref, nw_ref, h_ref, o_scr, l_scr):
    oa = oa_ref[...].astype(F32)
    gn = gn_ref[...]
    parts = [_rms(oa[:, h * GLA_DV:(h + 1) * GLA_DV], gn) for h in range(GLA_HEADS)]
    r = r_ref[...].astype(F32)
    oan = (jnp.concatenate(parts, axis=1) * (r * jax.nn.sigmoid(r))).astype(BF16)

    lses = [_tokens_from_streams(l, l_scr) for l in (l0_ref, l1_ref, l2_ref)]
    mx = jnp.maximum(jnp.maximum(lses[0], lses[1]), lses[2])
    es = [jnp.exp(l - mx) for l in lses]
    inv = 1.0 / (es[0] + es[1] + es[2])
    outs = [_tokens_from_streams(o, o_scr) for o in (o0_ref, o1_ref, o2_ref)]
    ob = []
    for h in range(DIL_HEADS):
        hs = slice(h * DIL_HD, (h + 1) * DIL_HD)
        acc = None
        for e, o in zip(es, outs):
            term = (e * inv)[:, h:h + 1] * o[:, hs]
            acc = term if acc is None else acc + term
        ob.append(acc)
    ob = jnp.concatenate(ob, axis=1).astype(BF16)

    merged = (ga_ref[...].astype(F32) * _dot(oan, wa_ref[...])
              + gb_ref[...].astype(F32) * _dot(ob, wb_ref[...]))
    y = _dot(merged.astype(BF16), wo_ref[...])
    h_ref[...] = x_ref[...] + _rms(y, nw_ref[...])


def _merge(x, oa, z, dil_o, dil_lse, gn, wa, wb, wo, nw, tm):
    m, d = x.shape
    va = oa.shape[1]
    zc = lambda c: pl.BlockSpec((tm, d), lambda i: (i, c * CW // d))
    row = lambda w: pl.BlockSpec((tm, w), lambda i: (i, 0))
    stream = lambda a: pl.BlockSpec((a.shape[0], tm // a.shape[0], a.shape[2]), lambda i: (0, i, 0))
    return pl.pallas_call(
        _merge_kernel,
        grid=(m // tm,),
        in_specs=[row(d), row(va), zc(Z_RA), zc(Z_GA), zc(Z_GB)]
        + [stream(a) for a in dil_o] + [stream(a) for a in dil_lse]
        + [_resident(gn.shape), _resident(wa.shape), _resident(wb.shape),
           _resident(wo.shape), _resident(nw.shape)],
        out_specs=row(d),
        out_shape=jax.ShapeDtypeStruct((m, d), F32),
        scratch_shapes=[pltpu.VMEM((CW // LANES, tm, LANES), F32), pltpu.VMEM((1, tm, LANES), F32)],
        compiler_params=_params(("parallel",)),
        name="merge",
    )(x, oa, z, z, z, *dil_o, *dil_lse, gn, wa, wb, wo, nw)


def _gelu_tanh(x):
    return 0.5 * x * (1.0 + jnp.tanh(0.7978845608028654 * (x + 0.044715 * x * x * x)))


def _ffn_kernel(*refs, fc, sample):
    if sample:
        h_ref, nw1_ref, wup_ref, wc_ref, bc_ref, wdn_ref, nw2_ref, p1_ref, p2_ref, out_ref, gst_ref = refs
    else:
        h_ref, nw1_ref, wup_ref, wc_ref, bc_ref, wdn_ref, nw2_ref, out_ref, gst_ref, cbuf = refs
    tm = h_ref.shape[0]
    dff = wdn_ref.shape[0]
    i = pl.program_id(0)

    if not sample:
        @pl.when(i == 0)
        def _():
            cbuf[...] = jnp.zeros_like(cbuf)

    h = h_ref[...]
    hn = _rms(h, nw1_ref[...]).astype(BF16)
    if sample:
        tpos = lax.broadcasted_iota(jnp.int32, (tm, 1), 0) % SROWS
    else:
        row8 = lax.broadcasted_iota(jnp.int32, (SUBLANES, 1), 0)
    acc = None
    for c in range(dff // fc):
        cs = slice(c * fc, (c + 1) * fc)
        gp = _dot(hn, wup_ref[:, cs])
        val = _dot(hn, wup_ref[:, dff + c * fc:dff + (c + 1) * fc])
        g1 = pltpu.roll(gp, 1, axis=0)
        g2 = pltpu.roll(gp, 2, axis=0)
        if sample:
            g1 = jnp.where(tpos == 0, p1_ref[:, cs], g1)
            g2 = jnp.where(tpos < 2, p2_ref[:, cs], g2)
            gst_ref[:, cs] = gp
        else:
            carry = cbuf[:, cs]
            head1 = jnp.where(row8 < 1, pltpu.roll(carry, 1, axis=0), g1[0:SUBLANES])
            head2 = jnp.where(row8 < 2, pltpu.roll(carry, 2, axis=0), g2[0:SUBLANES])
            g1 = jnp.concatenate([head1, g1[SUBLANES:]], axis=0)
            g2 = jnp.concatenate([head2, g2[SUBLANES:]], axis=0)
            cbuf[:, cs] = gp[tm - SUBLANES:]
        conv = g2 * wc_ref[0:1, cs] + g1 * wc_ref[1:2, cs] + gp * wc_ref[2:3, cs] + bc_ref[:, cs]
        act = (_gelu_tanh(conv) * val).astype(BF16)
        part = _dot(act, wdn_ref[cs, :])
        acc = part if acc is None else acc + part
    if not sample:
        gst_ref[...] = cbuf[...]
    out_ref[...] = h + _rms(acc, nw2_ref[...])


def _ffn(h, nw1, wup, wc, bc, wdn, nw2, prev, tm, fc):
    m, d = h.shape
    dff = wdn.shape[0]
    sample = prev is not None
    row = lambda w: pl.BlockSpec((tm, w), lambda i: (i, 0))
    in_specs = [row(d), _resident((1, d)), _resident(wup.shape), _resident(wc.shape),
                _resident((1, dff)), _resident(wdn.shape), _resident((1, d))]
    args = [h, nw1, wup, wc, bc, wdn, nw2]
    if sample:
        in_specs += [row(dff), row(dff)]
        args += list(prev)
        gst_spec, gst_shape = row(dff), (m, dff)
    else:
        gst_spec = pl.BlockSpec((SUBLANES, dff), lambda i: (i, 0))
        gst_shape = (m // tm * SUBLANES, dff)
    return pl.pallas_call(
        functools.partial(_ffn_kernel, fc=fc, sample=sample),
        grid=(m // tm,),
        in_specs=in_specs,
        out_specs=[row(d), gst_spec],
        out_shape=[jax.ShapeDtypeStruct((m, d), F32), jax.ShapeDtypeStruct(gst_shape, F32)],
        scratch_shapes=[] if sample else [pltpu.VMEM((SUBLANES, dff), F32)],
        compiler_params=_params(("arbitrary",)),
        name="ffn_sample" if sample else "ffn_prompt",
    )(*args)


def _rope_tables(pos):
    inv = ROPE_THETA ** (-jnp.arange(ROPE_HALF, dtype=F32) * 2.0 / ROPE_DIM)
    ang = pos[:, None] * inv[None, :]
    cos, sin = jnp.cos(ang), jnp.sin(ang)
    pad = LANES - ROPE_DIM
    one = jnp.ones((pos.shape[0], pad), F32)
    cos_t = jnp.concatenate([cos, cos, one], axis=1)
    sin_t = jnp.concatenate([-sin, sin, 0.0 * one], axis=1)
    return cos_t, sin_t


def _prep_in(w_in_l, d):
    qk = GLA_HEADS * GLA_DK
    va = GLA_HEADS * GLA_DV
    nb = N_GROUPS * DIL_OUT
    sizes = (qk, qk, va, va, GLA_RANK, 3 * nb, 2 * d)
    offs = np.cumsum((0,) + sizes)
    q_a, k_a, v_a, r_a, a_low, qkv_b, gate = (w_in_l[:, offs[j]:offs[j + 1]] for j in range(7))
    q_b, k_b, v_b = qkv_b[:, :nb], qkv_b[:, nb:2 * nb], qkv_b[:, 2 * nb:]
    wz = jnp.concatenate([q_a, k_a, v_a, r_a, gate, q_b, k_b, v_b], axis=1).astype(BF16)
    wal = jnp.pad(a_low, ((0, 0), (0, LANES - GLA_RANK))).astype(BF16)
    kv = []
    for g in range(N_GROUPS):
        kv += [k_b[:, g * DIL_OUT:(g + 1) * DIL_OUT], v_b[:, g * DIL_OUT:(g + 1) * DIL_OUT]]
    wkv = jnp.concatenate(kv, axis=1).astype(BF16)
    return wz, wal, wkv


def kernel(x_prompt, x_sample, state_gla, cache_dil_w128, cache_dil_w512, cache_dil_w2048, state_ffn_conv,
           w_in, w_gate_up, b_gate, gla_norm, w_branch_a, w_branch_b, w_out, norm_pre_mix, norm_post_mix,
           norm_pre_ffn, norm_post_ffn, w_ffn_up, w_conv, b_conv, w_ffn_down):
    depth, d = w_in.shape[0], w_in.shape[1]
    _, t_p, _ = x_prompt.shape
    nb, t_s, _ = x_sample.shape
    dff = w_ffn_down.shape[1]
    caches = (cache_dil_w128, cache_dil_w512, cache_dil_w2048)
    assert x_prompt.shape[0] == 1 and t_s <= SROWS // 2 and d == 2 * DIL_OUT
    for c, (win, _) in zip(caches, DIL_PAIRS):
        assert c.shape[2] == win and t_p >= win
    tail = max(w for w, _ in DIL_PAIRS)
    dils = tuple(dl for _, dl in DIL_PAIRS)
    tm = 256
    tm_big = min(512, t_p)
    m_s = nb * SROWS
    tm_s = min(tm, m_s)
    tm_ffn_s = min(128, m_s)

    xp = x_prompt.reshape(t_p, d)
    xs = jnp.pad(x_sample, ((0, 0), (0, SROWS - t_s), (0, 0))).reshape(nb * SROWS, d)
    cos_p, sin_p = _rope_tables(jnp.arange(t_p, dtype=F32))
    pos_s = PAST_LEN + jnp.minimum(jnp.arange(SROWS), t_s - 1).astype(F32)
    cos_s, sin_s = (jnp.tile(a, (nb, 1)) for a in _rope_tables(pos_s))
    cache_v = [c.reshape(depth, nb, c.shape[2] * 2 * DIL_HEADS, DIL_HD) for c in caches]
    new_cache = [None] * N_GROUPS
    bbs = (8, 4, 1)

    p_gla, p_kv, p_conv, s_gla, s_conv = [], [], [], [], []
    row2 = lambda a: a.reshape(1, -1)
    for i in range(depth):
        wz, wal, wkv = _prep_in(w_in[i], d)
        wgu = jnp.pad(w_gate_up[i], ((0, LANES - GLA_RANK), (0, 0))).astype(BF16)
        bg = row2(b_gate[i])
        wa, wb, wo = (w.astype(BF16) for w in (w_branch_a[i], w_branch_b[i], w_out[i]))
        wup, wdn = w_ffn_up[i].astype(BF16), w_ffn_down[i].astype(BF16)
        n1, n2, n3, n4 = (row2(n[i]) for n in (norm_pre_mix, norm_post_mix, norm_pre_ffn, norm_post_ffn))
        gn = row2(gla_norm[i])
        wc = jnp.pad(w_conv[i], ((0, SUBLANES - CONV_W), (0, 0)))
        bc = row2(b_conv[i])

        z, la, *qkv = _inproj(xp, n1, wz, wal, wgu, bg, cos_p, sin_p, tm_big, dils)
        kvt = _kvtail(xp[t_p - tail:], n1, wkv, cos_p[t_p - tail:], sin_p[t_p - tail:], tm)
        oa, sfin = _gla_prompt(z, la)
        dil = [_dil_prompt(a, min(1024, a.shape[1])) for a in qkv]
        h = _merge(xp, oa, z, [o for o, _ in dil], [l for _, l in dil], gn, wa, wb, wo, n2, tm)
        xp, gst = _ffn(h, n3, wup, wc, bc, wdn, n4, None, tm_big, 512)
        p_gla.append(sfin[None])
        p_kv.append(kvt)
        p_conv.append(gst[-(CONV_W - 1):][None])

        z, la, *qkv = _inproj(xs, n1, wz, wal, wgu, bg, cos_s, sin_s, tm_s, (1,) * N_GROUPS)
        kvn = _kvtail(xs, n1, wkv, cos_s, sin_s, tm_s)
        oa, sfin = _gla_sample(z, la, state_gla[i], t_s)
        dil_o, dil_l = [], []
        for g in range(N_GROUPS):
            qf = qkv[g][0, :, :CW].astype(F32)
            o, l, new_cache[g] = _dil_sample(qf, kvn, cache_v[g], new_cache[g], i, g, t_s, min(bbs[g], nb))
            dil_o.append(o[None])
            dil_l.append(l[None])
        h = _merge(xs, oa, z, dil_o, dil_l, gn, wa, wb, wo, n2, tm_s)
        st = state_ffn_conv[i]
        zpad = lambda a: jnp.pad(a, ((0, 0), (0, SROWS - a.shape[1]), (0, 0))).reshape(m_s, dff)
        prev = (zpad(st[:, 1:2]), zpad(st))
        xs, gpre = _ffn(h, n3, wup, wc, bc, wdn, n4, prev, tm_ffn_s, 512)
        s_gla.append(sfin)
        s_conv.append(gpre.reshape(nb, SROWS, dff)[:, t_s - (CONV_W - 1):t_s])

    y_p = xp.reshape(1, t_p, d)
    y_s = xs.reshape(nb, SROWS, d)[:, :t_s]
    p_caches = []
    for g, (win, _) in enumerate(DIL_PAIRS):
        rows = [kv[tail - win:, g * 2 * CW:(g + 1) * 2 * CW].reshape(1, win, 2, DIL_HEADS, DIL_HD) for kv in p_kv]
        p_caches.append(jnp.stack(rows))
    s_caches = [c.reshape(o.shape) for c, o in zip(new_cache, caches)]
    return (y_p, y_s, jnp.stack(p_gla), *p_caches, jnp.stack(p_conv),
            jnp.stack(s_gla), *s_caches, jnp.stack(s_conv))
```

```python
import functools

import numpy as np
import jax
import jax.numpy as jnp
from jax import lax
from jax.experimental import pallas as pl
from jax.experimental.pallas import tpu as pltpu

F32 = jnp.float32
BF16 = jnp.bfloat16

GLA_HEADS = 4
GLA_DK = 128
GLA_DV = 256
GLA_RANK = 16
GLA_TAU = 16.0
DIL_PAIRS = ((128, 1), (512, 4), (2048, 16))
N_GROUPS = len(DIL_PAIRS)
DIL_HEADS = 4
DIL_HD = 128
DIL_STEPS = 128
DIL_OUT = DIL_HEADS * DIL_HD
ROPE_THETA = 500000.0
ROPE_DIM = DIL_HD // 4
ROPE_HALF = ROPE_DIM // 2
CONV_W = 3
EPS = 1e-6
PAST_LEN = 16384

LANES = 128
SUBLANES = 8
VMEM_LIMIT = 56 * 1024 * 1024

CW = 512
Z_QA, Z_KA, Z_VA, Z_RA, Z_GA, Z_GB, Z_QB, Z_KB, Z_VB = 0, 1, 2, 4, 6, 8, 10, 13, 16
Z_BLOCKS = 19
SROWS = 8
DEC_T = 4
GLA_C = 128
NEG = -1e30
COPY_ROWS = 32


def _params(sem):
    return pltpu.CompilerParams(dimension_semantics=sem, vmem_limit_bytes=VMEM_LIMIT)


def _resident(shape):
    nd = len(shape)
    return pl.BlockSpec(shape, lambda *_: (0,) * nd, pipeline_mode=pl.Buffered(1))


def _rms(x, w):
    ms = jnp.mean(x * x, axis=-1, keepdims=True)
    return x * lax.rsqrt(ms + EPS) * w


def _dot(a, b):
    return jnp.dot(a, b, preferred_element_type=F32)


def _dot_nt(a, b):
    return lax.dot_general(a, b, (((1,), (1,)), ((), ())), preferred_element_type=F32)


def _rope_heads(acc, cos_t, sin_t, lt):
    parts = []
    for h in range(DIL_HEADS):
        a = acc[:, h * LANES:(h + 1) * LANES]
        up = pltpu.roll(a, LANES - ROPE_HALF, axis=1)
        dn = pltpu.roll(a, ROPE_HALF, axis=1)
        parts.append(a * cos_t + jnp.where(lt, up, dn) * sin_t)
    return parts


def _inproj_kernel(x_ref, nw_ref, wz_ref, wal_ref, wgu_ref, bg_ref, cos_ref, sin_ref,
                   z_ref, la_ref, g0_ref, g1_ref, g2_ref, scr, *, dils):
    tm = x_ref.shape[0]
    xn = _rms(x_ref[...], nw_ref[...]).astype(BF16)
    cos_t = cos_ref[...]
    sin_t = sin_ref[...]
    lt = lax.broadcasted_iota(jnp.int32, cos_t.shape, 1) < ROPE_HALF
    grefs = (g0_ref, g1_ref, g2_ref)
    for c in range(Z_BLOCKS):
        acc = _dot(xn, wz_ref[:, c * CW:(c + 1) * CW])
        if c < Z_QB:
            if c == Z_QA:
                acc = acc * (GLA_DK ** -0.5)
            elif c >= Z_GA:
                acc = jax.nn.sigmoid(acc)
            z_ref[:, c * CW:(c + 1) * CW] = acc.astype(BF16)
            continue
        part, g = divmod(c - Z_QB, N_GROUPS)
        if part < 2:
            acc = jnp.concatenate(_rope_heads(acc, cos_t, sin_t, lt), axis=1)
        if part == 0:
            acc = acc * (DIL_HD ** -0.5)
        cols = slice(part * CW, (part + 1) * CW)
        dil = dils[g]
        if dil == 1:
            grefs[g][0, :, cols] = acc.astype(BF16)
        else:
            for j in range(CW // LANES):
                scr[j] = acc[:, j * LANES:(j + 1) * LANES]
            for r in range(dil):
                rows = [scr[j, pl.ds(r, tm // dil, stride=dil), :] for j in range(CW // LANES)]
                grefs[g][r, :, cols] = jnp.concatenate(rows, axis=1).astype(BF16)
    al = _dot(xn, wal_ref[...]).astype(BF16)
    lg = _dot(al, wgu_ref[...]) + bg_ref[...]
    la_ref[...] = (jnp.minimum(lg, 0.0) - jnp.log1p(jnp.exp(-jnp.abs(lg)))) * (1.0 / GLA_TAU)


def _inproj(x, nw, wz, wal, wgu, bg, cos_t, sin_t, tm, dils):
    m, d = x.shape
    nw_all = wz.shape[1]
    nz = Z_QB * CW
    qk = wgu.shape[1]
    row = lambda w: pl.BlockSpec((tm, w), lambda i: (i, 0))
    gspec = lambda dl: pl.BlockSpec((dl, tm // dl, 3 * CW), lambda i: (0, i, 0))
    return pl.pallas_call(
        functools.partial(_inproj_kernel, dils=dils),
        grid=(m // tm,),
        in_specs=[row(d), _resident((1, d)), _resident((d, nw_all)), _resident((d, LANES)),
                  _resident((LANES, qk)), _resident((1, qk)), row(LANES), row(LANES)],
        out_specs=[row(nz), row(qk)] + [gspec(dl) for dl in dils],
        out_shape=[jax.ShapeDtypeStruct((m, nz), BF16), jax.ShapeDtypeStruct((m, qk), F32)]
        + [jax.ShapeDtypeStruct((dl, m // dl, 3 * CW), BF16) for dl in dils],
        scratch_shapes=[pltpu.VMEM((CW // LANES, tm, LANES), F32)],
        compiler_params=_params(("parallel",)),
        name="inproj",
    )(x, nw, wz, wal, wgu, bg, cos_t, sin_t)


def _kvtail_kernel(x_ref, nw_ref, w_ref, cos_ref, sin_ref, o_ref):
    xn = _rms(x_ref[...], nw_ref[...]).astype(BF16)
    cos_t = cos_ref[...]
    sin_t = sin_ref[...]
    lt = lax.broadcasted_iota(jnp.int32, cos_t.shape, 1) < ROPE_HALF
    for c in range(2 * N_GROUPS):
        acc = _dot(xn, w_ref[:, c * CW:(c + 1) * CW])
        if c % 2 == 0:
            for h, p in enumerate(_rope_heads(acc, cos_t, sin_t, lt)):
                o_ref[:, c * CW + h * LANES:c * CW + (h + 1) * LANES] = p
        else:
            o_ref[:, c * CW:(c + 1) * CW] = acc


def _kvtail(x, nw, wkv, cos_t, sin_t, tm):
    m, d = x.shape
    n = wkv.shape[1]
    row = lambda w: pl.BlockSpec((tm, w), lambda i: (i, 0))
    return pl.pallas_call(
        _kvtail_kernel,
        grid=(m // tm,),
        in_specs=[row(d), _resident((1, d)), _resident((d, n)), row(LANES), row(LANES)],
        out_specs=row(n),
        out_shape=jax.ShapeDtypeStruct((m, n), F32),
        compiler_params=_params(("parallel",)),
        name="kvtail",
    )(x, nw, wkv, cos_t, sin_t)


@functools.lru_cache(maxsize=None)
def _gla_consts(C, nl, seg):
    x = np.arange(C)[:, None]
    i = np.arange(C)[None, :]
    same = (x // seg == i // seg) if seg else np.ones((C, C), bool)
    mats = [(i <= x) & same, (i > x) & same]
    masks = [x == i]
    for l in range(nl):
        m = 1 << l
        r = (x | (2 * m - 1)) - m
        mats.append((i > np.minimum(x, r)) & (i <= np.maximum(x, r)))
        masks.append((((x >> l) ^ (i >> l)) == 1) & (((x >> l) & 1) == 1))
    pall = np.concatenate(mats, 0).astype(np.float32)
    return pall, np.stack(masks).astype(np.float32)


def _gla_core(q, k, v, g, pall_ref, mask_ref, nl):
    C = q.shape[0]
    g_hi = g.astype(BF16)
    g_lo = (g - g_hi.astype(F32)).astype(BF16)
    ex = _dot(pall_ref[...], jnp.concatenate([g_hi, g_lo], axis=1))
    ex = ex[:, :GLA_DK] + ex[:, GLA_DK:]
    b = ex[0:C]
    aft = ex[C:2 * C]
    qf = q.astype(F32)
    kf = k.astype(F32)
    a = _dot_nt(q, k) * mask_ref[0]
    for l in range(nl):
        e = jnp.exp(ex[(2 + l) * C:(3 + l) * C])
        a = a + _dot_nt((qf * e).astype(BF16), (kf * e).astype(BF16)) * mask_ref[1 + l]
    o = _dot(a.astype(BF16), v)
    qe = (qf * jnp.exp(b)).astype(BF16)
    kdt = (kf * jnp.exp(aft)).T.astype(BF16)
    return o, qe, kdt, b.T


def _gla_prompt_kernel(q_ref, k_ref, v_ref, g_ref, pall_ref, mask_ref, o_ref, sf_ref, s_scr, *, nl):
    i = pl.program_id(0)

    @pl.when(i == 0)
    def _():
        s_scr[...] = jnp.zeros_like(s_scr)

    C = q_ref.shape[0]
    for h in range(GLA_HEADS):
        ks = slice(h * GLA_DK, (h + 1) * GLA_DK)
        vs = slice(h * GLA_DV, (h + 1) * GLA_DV)
        v = v_ref[:, vs]
        o, qe, kdt, bt = _gla_core(q_ref[:, ks], k_ref[:, ks], v, g_ref[:, ks], pall_ref, mask_ref, nl)
        s = s_scr[h]
        o_ref[:, vs] = (o + _dot(qe, s.astype(BF16))).astype(BF16)
        s_scr[h] = s * jnp.exp(bt[:, C - 1:C]) + _dot(kdt, v)

    @pl.when(i == pl.num_programs(0) - 1)
    def _():
        sf_ref[...] = s_scr[...]


def _gla_prompt(z, la):
    m = z.shape[0]
    C = GLA_C
    nl = C.bit_length() - 1
    pall, masks = _gla_consts(C, nl, 0)
    pall = jnp.asarray(pall, BF16)
    masks = jnp.asarray(masks, F32)
    qk = GLA_HEADS * GLA_DK
    va = GLA_HEADS * GLA_DV
    return pl.pallas_call(
        functools.partial(_gla_prompt_kernel, nl=nl),
        grid=(m // C,),
        in_specs=[pl.BlockSpec((C, qk), lambda i: (i, Z_QA)),
                  pl.BlockSpec((C, qk), lambda i: (i, Z_KA)),
                  pl.BlockSpec((C, va), lambda i: (i, Z_VA * CW // va)),
                  pl.BlockSpec((C, qk), lambda i: (i, 0)),
                  _resident(pall.shape), _resident(masks.shape)],
        out_specs=[pl.BlockSpec((C, va), lambda i: (i, 0)),
                   pl.BlockSpec((GLA_HEADS, GLA_DK, GLA_DV), lambda i: (0, 0, 0))],
        out_shape=[jax.ShapeDtypeStruct((m, va), BF16),
                   jax.ShapeDtypeStruct((GLA_HEADS, GLA_DK, GLA_DV), F32)],
        scratch_shapes=[pltpu.VMEM((GLA_HEADS, GLA_DK, GLA_DV), F32)],
        compiler_params=_params(("arbitrary",)),
        name="gla_prompt",
    )(z, z, z, la, pall, masks)


def _gla_sample_kernel(q_ref, k_ref, v_ref, g_ref, s0_ref, pall_ref, mask_ref, o_ref, sf_ref, *, nl, tvalid):
    C = q_ref.shape[0]
    nseq = C // SROWS
    row = lax.broadcasted_iota(jnp.int32, (C, 1), 0)
    g = jnp.where(row % SROWS < tvalid, g_ref[...], 0.0)
    v = v_ref[...]
    o, qe, kdt, bt = _gla_core(q_ref[...], k_ref[...], v, g, pall_ref, mask_ref, nl)
    zero_v = jnp.zeros_like(v)
    for j in range(nseq):
        mine = row // SROWS == j
        s = s0_ref[j, 0]
        o = o + jnp.where(mine, _dot(qe, s.astype(BF16)), 0.0)
        last = j * SROWS + SROWS - 1
        sf_ref[j, 0] = s * jnp.exp(bt[:, last:last + 1]) + _dot(kdt, jnp.where(mine, v, zero_v))
    o_ref[...] = o.astype(BF16)


def _gla_sample(z, la, s0, tvalid):
    m = z.shape[0]
    C = GLA_C
    nseq = C // SROWS
    nl = (tvalid - 1).bit_length()
    pall, masks = _gla_consts(C, nl, SROWS)
    pall = jnp.asarray(pall, BF16)
    masks = jnp.asarray(masks, F32)
    va = GLA_HEADS * GLA_DV
    sblk = (nseq, 1, GLA_DK, GLA_DV)
    return pl.pallas_call(
        functools.partial(_gla_sample_kernel, nl=nl, tvalid=tvalid),
        grid=(m // C, GLA_HEADS),
        in_specs=[pl.BlockSpec((C, GLA_DK), lambda c, h: (c, Z_QA * CW // GLA_DK + h)),
                  pl.BlockSpec((C, GLA_DK), lambda c, h: (c, Z_KA * CW // GLA_DK + h)),
                  pl.BlockSpec((C, GLA_DV), lambda c, h: (c, Z_VA * CW // GLA_DV + h)),
                  pl.BlockSpec((C, GLA_DK), lambda c, h: (c, h)),
                  pl.BlockSpec(sblk, lambda c, h: (c, h, 0, 0)),
                  _resident(pall.shape), _resident(masks.shape)],
        out_specs=[pl.BlockSpec((C, GLA_DV), lambda c, h: (c, h)),
                   pl.BlockSpec(sblk, lambda c, h: (c, h, 0, 0))],
        out_shape=[jax.ShapeDtypeStruct((m, va), BF16),
                   jax.ShapeDtypeStruct(s0.shape, F32)],
        compiler_params=_params(("parallel", "parallel")),
        name="gla_sample",
    )(z, z, z, la, s0, pall, masks)


def _dil_prompt_kernel(q_ref, k_ref, kp_ref, v_ref, vp_ref, o_ref, lse_ref):
    i = pl.program_id(1)
    rb = q_ref.shape[0]
    nk = 2 * DIL_STEPS
    qi = lax.broadcasted_iota(jnp.int32, (DIL_STEPS, nk), 0)
    ki = lax.broadcasted_iota(jnp.int32, (DIL_STEPS, nk), 1)
    lane = lax.broadcasted_iota(jnp.int32, (DIL_STEPS, LANES), 1)
    band = (ki >= qi) & (ki <= qi + DIL_STEPS)
    bias = jnp.where(band, 0.0, NEG)
    bias_first = jnp.where((i > 0) | (ki >= DIL_STEPS), bias, NEG)
    ones = jnp.ones((nk, DIL_HD), BF16)
    for sb in range(rb // DIL_STEPS):
        rows = slice(sb * DIL_STEPS, (sb + 1) * DIL_STEPS)
        lse_tile = jnp.zeros((DIL_STEPS, LANES), F32)
        for h in range(DIL_HEADS):
            hs = slice(h * DIL_HD, (h + 1) * DIL_HD)
            if sb == 0:
                kcat = jnp.concatenate([kp_ref[:, hs], k_ref[rows, hs]], axis=0)
                vcat = jnp.concatenate([vp_ref[:, hs], v_ref[rows, hs]], axis=0)
            else:
                krows = slice((sb - 1) * DIL_STEPS, (sb + 1) * DIL_STEPS)
                kcat, vcat = k_ref[krows, hs], v_ref[krows, hs]
            s = _dot_nt(q_ref[rows, hs], kcat) + (bias_first if sb == 0 else bias)
            mx = jnp.max(s, axis=-1, keepdims=True)
            p = jnp.exp(s - mx).astype(BF16)
            pv = _dot(p, jnp.concatenate([vcat, ones], axis=1))
            den = pv[:, DIL_HD:]
            o_ref[rows, hs] = (pv[:, :DIL_HD] / den).astype(BF16)
            lse_tile = jnp.where(lane == h, mx + jnp.log(den), lse_tile)
        lse_ref[rows, :] = lse_tile


def _dil_prompt(qkv, rb):
    dil, l, _ = qkv.shape
    nprev = rb // DIL_STEPS
    own = lambda c: pl.BlockSpec((None, rb, CW), lambda r, i: (r, i, c))
    prev = lambda c: pl.BlockSpec((None, DIL_STEPS, CW), lambda r, i: (r, jnp.maximum(i * nprev - 1, 0), c))
    return pl.pallas_call(
        _dil_prompt_kernel,
        grid=(dil, l // rb),
        in_specs=[own(0), own(1), prev(1), own(2), prev(2)],
        out_specs=[pl.BlockSpec((None, rb, CW), lambda r, i: (r, i, 0)),
                   pl.BlockSpec((None, rb, LANES), lambda r, i: (r, i, 0))],
        out_shape=[jax.ShapeDtypeStruct((dil, l, CW), BF16),
                   jax.ShapeDtypeStruct((dil, l, LANES), F32)],
        compiler_params=_params(("parallel", "parallel")),
        name=f"dil_prompt_d{dil}",
    )(qkv, qkv, qkv, qkv, qkv)


NUNIT = DEC_T * DIL_HEADS
KPAD = 2 * DIL_STEPS


def _dil_sample_kernel(q_ref, kvn_ref, c0_ref, c1_ref, c2_ref, a0_ref, a1_ref, a2_ref,
                       o_ref, n0_ref, n1_ref, n2_ref, *, bb, tvalid):
    del a0_ref, a1_ref, a2_ref
    rpt = 2 * DIL_HEADS
    qall = q_ref[...]
    kvn = kvn_ref[...]
    row_u = lax.broadcasted_iota(jnp.int32, (NUNIT, KPAD), 0)
    key = lax.broadcasted_iota(jnp.int32, (NUNIT, KPAD), 1)
    t_u = row_u // DIL_HEADS
    newt = key - DIL_STEPS
    bias_d1 = jnp.where(key < DIL_STEPS, jnp.where(key >= t_u, 0.0, NEG), jnp.where(newt <= t_u, 0.0, NEG))
    bias_dn = jnp.where(key < DIL_STEPS, 0.0, jnp.where(newt == t_u, 0.0, NEG))
    blk_of_lane = lax.broadcasted_iota(jnp.int32, (NUNIT, NUNIT * DIL_HD), 1) // DIL_HD
    diag = lax.broadcasted_iota(jnp.int32, (NUNIT, NUNIT * DIL_HD), 0) == blk_of_lane
    row16 = lax.broadcasted_iota(jnp.int32, (NUNIT, DIL_HD), 0)
    zpad = jnp.zeros((KPAD - DIL_STEPS - 2 * SROWS, NUNIT * DIL_HD), BF16)
    zrow = jnp.zeros((SROWS, NUNIT * DIL_HD), F32)
    c_refs = (c0_ref, c1_ref, c2_ref)
    n_refs = (n0_ref, n1_ref, n2_ref)

    for b in range(bb):
        rows8 = slice(b * SROWS, (b + 1) * SROWS)
        outs, lses = [], []
        for g, (_, dil) in enumerate(DIL_PAIRS):
            c_ref = c_refs[g]
            knew = kvn[rows8, g * 2 * CW:g * 2 * CW + CW]
            vnew = kvn[rows8, g * 2 * CW + CW:(g + 1) * 2 * CW]
            for t in range(tvalid):
                r = b * SROWS + t
                n_refs[g][b, t * rpt:(t + 1) * rpt, :] = jnp.concatenate(
                    [kvn[r:r + 1, g * 2 * CW + j * DIL_HD:g * 2 * CW + (j + 1) * DIL_HD] for j in range(rpt)],
                    axis=0)
            q16 = [jnp.concatenate(
                [jnp.broadcast_to(qall[b * SROWS + t:b * SROWS + t + 1, g * CW + h * DIL_HD:g * CW + (h + 1) * DIL_HD],
                                  (DIL_HEADS, DIL_HD)) for t in range(DEC_T)], axis=0) for h in range(DIL_HEADS)]
            kb, vb, qb = {}, {}, []
            for u in range(NUNIT):
                t, h = divmod(u, DIL_HEADS)
                first = (t if dil > 1 else 0) * rpt + h
                if first not in kb:
                    kb[first] = c_ref[b, pl.ds(first, DIL_STEPS, stride=rpt * dil), :].astype(BF16)
                    vb[first] = c_ref[b, pl.ds(first + DIL_HEADS, DIL_STEPS, stride=rpt * dil), :].astype(BF16)
                qb.append(jnp.where(row16 == u, q16[h], 0.0))
            firsts = [(divmod(u, DIL_HEADS)[0] if dil > 1 else 0) * rpt + u % DIL_HEADS for u in range(NUNIT)]
            hcols = lambda a: jnp.concatenate([a[:, (u % DIL_HEADS) * DIL_HD:(u % DIL_HEADS + 1) * DIL_HD]
                                               for u in range(NUNIT)], axis=1)
            k_ext = jnp.concatenate([jnp.concatenate([kb[f] for f in firsts], axis=1),
                                     jnp.concatenate([hcols(knew), zrow], axis=0).astype(BF16), zpad], axis=0)
            v_ext = jnp.concatenate([jnp.concatenate([vb[f] for f in firsts], axis=1),
                                     jnp.concatenate([hcols(vnew), zrow], axis=0).astype(BF16), zpad], axis=0)
            qbd = jnp.concatenate(qb, axis=1).astype(BF16)
            s = _dot_nt(qbd, k_ext) + (bias_d1 if dil == 1 else bias_dn)
            mx = jnp.max(s, axis=-1, keepdims=True)
            p = jnp.exp(s - mx)
            den = jnp.sum(p, axis=-1, keepdims=True)
            outs.append(_dot(p.astype(BF16), v_ext) / den)
            lses.append(mx + jnp.log(den))
        mxl = jnp.maximum(jnp.maximum(lses[0], lses[1]), lses[2])
        es = [jnp.exp(l - mxl) for l in lses]
        inv = 1.0 / (es[0] + es[1] + es[2])
        merged = (es[0] * inv) * outs[0] + (es[1] * inv) * outs[1] + (es[2] * inv) * outs[2]
        red = jnp.sum(jnp.where(diag, merged, 0.0), axis=0, keepdims=True)
        orow = [red[:, t * DIL_OUT:(t + 1) * DIL_OUT] for t in range(DEC_T)]
        o_ref[rows8, :] = jnp.concatenate(orow + [jnp.zeros((SROWS - DEC_T, DIL_OUT), F32)], axis=0)


def _dil_sample(qf, kvn, caches, rolled, layer, tvalid, bb):
    assert tvalid == DEC_T
    m = qf.shape[0]
    nb = caches[0].shape[1]
    rows = bb * SROWS
    rpt = 2 * DIL_HEADS
    cspecs, nspecs = [], []
    for c in caches:
        wr = c.shape[2]
        cspecs.append(pl.BlockSpec((None, bb, wr, LANES), lambda i: (layer, i, 0, 0)))
        nspecs.append(pl.BlockSpec((None, bb, tvalid * rpt, LANES),
                                   lambda i, last=wr // (tvalid * rpt) - 1: (layer, i, last, 0)))
    anyspec = pl.BlockSpec(memory_space=pl.ANY)
    outs = pl.pallas_call(
        functools.partial(_dil_sample_kernel, bb=bb, tvalid=tvalid),
        grid=(nb // bb,),
        in_specs=[pl.BlockSpec((rows, N_GROUPS * CW), lambda i: (i, 0)),
                  pl.BlockSpec((rows, N_GROUPS * 2 * CW), lambda i: (i, 0))] + cspecs + [anyspec] * N_GROUPS,
        out_specs=[pl.BlockSpec((rows, CW), lambda i: (i, 0))] + nspecs,
        out_shape=[jax.ShapeDtypeStruct((m, CW), F32)] + [jax.ShapeDtypeStruct(r.shape, F32) for r in rolled],
        input_output_aliases={2 + N_GROUPS + g: 1 + g for g in range(N_GROUPS)},
        compiler_params=_params(("parallel",)),
        name="dil_sample",
    )(qf, kvn, *caches, *rolled)
    return outs[0], list(outs[1:])


def _tokens_from_streams(ref, scr):
    dil, n, w = ref.shape
    if dil == 1:
        return ref[0].astype(F32)
    nslab = w // LANES
    for r in range(dil):
        blk = ref[r].astype(F32)
        for j in range(nslab):
            scr[j, pl.ds(r, n, stride=dil), :] = blk[:, j * LANES:(j + 1) * LANES]
    return jnp.concatenate([scr[j] for j in range(nslab)], axis=1)


def _merge_kernel(*refs, grouped):
    if grouped:
        (x_ref, oa_ref, r_ref, ga_ref, gb_ref, o0_ref, o1_ref, o2_ref, l0_ref, l1_ref, l2_ref,
         gn_ref, wa_ref, wb_ref, wo_ref, nw_ref, h_ref, o_scr, l_scr) = refs
    else:
        x_ref, oa_ref, r_ref, ga_ref, gb_ref, ob_ref, gn_ref, wa_ref, wb_ref, wo_ref, nw_ref, h_ref = refs
    oa = oa_ref[...].astype(F32)
    gn = gn_ref[...]
    parts = [_rms(oa[:, h * GLA_DV:(h + 1) * GLA_DV], gn) for h in range(GLA_HEADS)]
    r = r_ref[...].astype(F32)
    oan = (jnp.concatenate(parts, axis=1) * (r * jax.nn.sigmoid(r))).astype(BF16)

    if grouped:
        lses = [_tokens_from_streams(l, l_scr) for l in (l0_ref, l1_ref, l2_ref)]
        mx = jnp.maximum(jnp.maximum(lses[0], lses[1]), lses[2])
        es = [jnp.exp(l - mx) for l in lses]
        inv = 1.0 / (es[0] + es[1] + es[2])
        outs = [_tokens_from_streams(o, o_scr) for o in (o0_ref, o1_ref, o2_ref)]
        ob = []
        for h in range(DIL_HEADS):
            hs = slice(h * DIL_HD, (h + 1) * DIL_HD)
            acc = None
            for e, o in zip(es, outs):
                term = (e * inv)[:, h:h + 1] * o[:, hs]
                acc = term if acc is None else acc + term
            ob.append(acc)
        ob = jnp.concatenate(ob, axis=1).astype(BF16)
    else:
        ob = ob_ref[...].astype(BF16)

    merged = (ga_ref[...].astype(F32) * _dot(oan, wa_ref[...])
              + gb_ref[...].astype(F32) * _dot(ob, wb_ref[...]))
    y = _dot(merged.astype(BF16), wo_ref[...])
    h_ref[...] = x_ref[...] + _rms(y, nw_ref[...])


def _merge(x, oa, z, dil_o, dil_lse, gn, wa, wb, wo, nw, tm):
    m, d = x.shape
    va = oa.shape[1]
    grouped = dil_lse is not None
    zc = lambda c: pl.BlockSpec((tm, d), lambda i: (i, c * CW // d))
    row = lambda w: pl.BlockSpec((tm, w), lambda i: (i, 0))
    stream = lambda a: pl.BlockSpec((a.shape[0], tm // a.shape[0], a.shape[2]), lambda i: (0, i, 0))
    if grouped:
        dil_specs = [stream(a) for a in dil_o] + [stream(a) for a in dil_lse]
        dil_args = [*dil_o, *dil_lse]
        scratch = [pltpu.VMEM((CW // LANES, tm, LANES), F32), pltpu.VMEM((1, tm, LANES), F32)]
    else:
        dil_specs, dil_args, scratch = [row(CW)], [dil_o], []
    return pl.pallas_call(
        functools.partial(_merge_kernel, grouped=grouped),
        grid=(m // tm,),
        in_specs=[row(d), row(va), zc(Z_RA), zc(Z_GA), zc(Z_GB)] + dil_specs
        + [_resident(gn.shape), _resident(wa.shape), _resident(wb.shape),
           _resident(wo.shape), _resident(nw.shape)],
        out_specs=row(d),
        out_shape=jax.ShapeDtypeStruct((m, d), F32),
        scratch_shapes=scratch,
        compiler_params=_params(("parallel",)),
        name="merge",
    )(x, oa, z, z, z, *dil_args, gn, wa, wb, wo, nw)


def _gelu_tanh(x):
    return 0.5 * x * (1.0 + jnp.tanh(0.7978845608028654 * (x + 0.044715 * x * x * x)))


ROLL_CHUNKS = (1, 2, 8)


def _n_roll_copies(depth):
    return depth * (sum(ROLL_CHUNKS) + N_GROUPS)


def _roll_copies(c_refs, o_refs, sem, drop_rows):
    copies = []
    for c_ref, o_ref, nchunk in zip(c_refs, o_refs, ROLL_CHUNKS):
        depth, nb, rows, _ = c_ref.shape
        per = nb // nchunk
        keep = rows - drop_rows
        for layer in range(depth):
            for q in range(nchunk):
                copies.append(pltpu.make_async_copy(
                    c_ref.at[layer, pl.ds(q * per, per), pl.ds(drop_rows, keep), :],
                    o_ref.at[layer, pl.ds(q * per, per), pl.ds(0, keep), :],
                    sem.at[len(copies)]))
            copies.append(pltpu.make_async_copy(
                c_ref.at[layer, :, pl.ds(keep, drop_rows), :],
                o_ref.at[layer, :, pl.ds(keep, drop_rows), :],
                sem.at[len(copies)]))
    return copies


def _ffn_kernel(*refs, fc, sample, roll):
    if sample:
        h_ref, nw1_ref, wup_ref, wc_ref, bc_ref, wdn_ref, nw2_ref, p1_ref, p2_ref, out_ref, gst_ref = refs
    elif roll is None:
        h_ref, nw1_ref, wup_ref, wc_ref, bc_ref, wdn_ref, nw2_ref, out_ref, gst_ref, cbuf = refs
    else:
        drop_rows = roll
        (h_ref, nw1_ref, wup_ref, wc_ref, bc_ref, wdn_ref, nw2_ref), rest = refs[:7], refs[7:]
        c_refs, rest = rest[:N_GROUPS], rest[N_GROUPS:]
        out_ref, gst_ref = rest[:2]
        o_refs = rest[2:2 + N_GROUPS]
        cbuf, sem = rest[2 + N_GROUPS:]
    tm = h_ref.shape[0]
    dff = wdn_ref.shape[0]
    i = pl.program_id(0)

    if not sample:
        @pl.when(i == 0)
        def _():
            cbuf[...] = jnp.zeros_like(cbuf)
            if roll is not None:
                for cp in _roll_copies(c_refs, o_refs, sem, drop_rows):
                    cp.start()

    h = h_ref[...]
    hn = _rms(h, nw1_ref[...]).astype(BF16)
    if sample:
        tpos = lax.broadcasted_iota(jnp.int32, (tm, 1), 0) % SROWS
    else:
        row8 = lax.broadcasted_iota(jnp.int32, (SUBLANES, 1), 0)
    acc = None
    for c in range(dff // fc):
        cs = slice(c * fc, (c + 1) * fc)
        gp = _dot(hn, wup_ref[:, cs])
        val = _dot(hn, wup_ref[:, dff + c * fc:dff + (c + 1) * fc])
        g1 = pltpu.roll(gp, 1, axis=0)
        g2 = pltpu.roll(gp, 2, axis=0)
        if sample:
            g1 = jnp.where(tpos == 0, p1_ref[:, cs], g1)
            g2 = jnp.where(tpos < 2, p2_ref[:, cs], g2)
            gst_ref[:, cs] = gp
        else:
            carry = cbuf[:, cs]
            head1 = jnp.where(row8 < 1, pltpu.roll(carry, 1, axis=0), g1[0:SUBLANES])
            head2 = jnp.where(row8 < 2, pltpu.roll(carry, 2, axis=0), g2[0:SUBLANES])
            g1 = jnp.concatenate([head1, g1[SUBLANES:]], axis=0)
            g2 = jnp.concatenate([head2, g2[SUBLANES:]], axis=0)
            cbuf[:, cs] = gp[tm - SUBLANES:]
        conv = g2 * wc_ref[0:1, cs] + g1 * wc_ref[1:2, cs] + gp * wc_ref[2:3, cs] + bc_ref[:, cs]
        act = (_gelu_tanh(conv) * val).astype(BF16)
        part = _dot(act, wdn_ref[cs, :])
        acc = part if acc is None else acc + part
    if not sample:
        gst_ref[...] = cbuf[...]
    out_ref[...] = h + _rms(acc, nw2_ref[...])

    if roll is not None:
        @pl.when(i == pl.num_programs(0) - 1)
        def _():
            for cp in _roll_copies(c_refs, o_refs, sem, drop_rows):
                cp.wait()


def _ffn(h, nw1, wup, wc, bc, wdn, nw2, prev, tm, fc, roll=None):
    m, d = h.shape
    dff = wdn.shape[0]
    sample = prev is not None
    row = lambda w: pl.BlockSpec((tm, w), lambda i: (i, 0))
    anyspec = pl.BlockSpec(memory_space=pl.ANY)
    in_specs = [row(d), _resident((1, d)), _resident(wup.shape), _resident(wc.shape),
                _resident((1, dff)), _resident(wdn.shape), _resident((1, d))]
    args = [h, nw1, wup, wc, bc, wdn, nw2]
    out_specs, out_shape, scratch, kroll = [row(d)], [jax.ShapeDtypeStruct((m, d), F32)], [], None
    if sample:
        in_specs += [row(dff), row(dff)]
        args += list(prev)
        out_specs.append(row(dff))
        out_shape.append(jax.ShapeDtypeStruct((m, dff), F32))
    else:
        out_specs.append(pl.BlockSpec((SUBLANES, dff), lambda i: (i, 0)))
        out_shape.append(jax.ShapeDtypeStruct((m // tm * SUBLANES, dff), F32))
        scratch.append(pltpu.VMEM((SUBLANES, dff), F32))
    if roll is not None:
        kroll, caches = roll
        in_specs += [anyspec] * len(caches)
        args += list(caches)
        out_specs += [anyspec] * len(caches)
        out_shape += [jax.ShapeDtypeStruct(c.shape, F32) for c in caches]
        scratch.append(pltpu.SemaphoreType.DMA((_n_roll_copies(caches[0].shape[0]),)))
    outs = pl.pallas_call(
        functools.partial(_ffn_kernel, fc=fc, sample=sample, roll=kroll),
        grid=(m // tm,),
        in_specs=in_specs,
        out_specs=out_specs,
        out_shape=out_shape,
        scratch_shapes=scratch,
        compiler_params=_params(("arbitrary",)),
        name="ffn_sample" if sample else "ffn_prompt",
    )(*args)
    return outs[0], outs[1], list(outs[2:])


def _rope_tables(pos):
    inv = ROPE_THETA ** (-jnp.arange(ROPE_HALF, dtype=F32) * 2.0 / ROPE_DIM)
    ang = pos[:, None] * inv[None, :]
    cos, sin = jnp.cos(ang), jnp.sin(ang)
    pad = LANES - ROPE_DIM
    one = jnp.ones((pos.shape[0], pad), F32)
    cos_t = jnp.concatenate([cos, cos, one], axis=1)
    sin_t = jnp.concatenate([-sin, sin, 0.0 * one], axis=1)
    return cos_t, sin_t


def _prep_in(w_in_l, d):
    qk = GLA_HEADS * GLA_DK
    va = GLA_HEADS * GLA_DV
    nb = N_GROUPS * DIL_OUT
    sizes = (qk, qk, va, va, GLA_RANK, 3 * nb, 2 * d)
    offs = np.cumsum((0,) + sizes)
    q_a, k_a, v_a, r_a, a_low, qkv_b, gate = (w_in_l[:, offs[j]:offs[j + 1]] for j in range(7))
    q_b, k_b, v_b = qkv_b[:, :nb], qkv_b[:, nb:2 * nb], qkv_b[:, 2 * nb:]
    wz = jnp.concatenate([q_a, k_a, v_a, r_a, gate, q_b, k_b, v_b], axis=1).astype(BF16)
    wal = jnp.pad(a_low, ((0, 0), (0, LANES - GLA_RANK))).astype(BF16)
    kv = []
    for g in range(N_GROUPS):
        kv += [k_b[:, g * DIL_OUT:(g + 1) * DIL_OUT], v_b[:, g * DIL_OUT:(g + 1) * DIL_OUT]]
    wkv = jnp.concatenate(kv, axis=1).astype(BF16)
    return wz, wal, wkv


def kernel(x_prompt, x_sample, state_gla, cache_dil_w128, cache_dil_w512, cache_dil_w2048, state_ffn_conv,
           w_in, w_gate_up, b_gate, gla_norm, w_branch_a, w_branch_b, w_out, norm_pre_mix, norm_post_mix,
           norm_pre_ffn, norm_post_ffn, w_ffn_up, w_conv, b_conv, w_ffn_down):
    depth, d = w_in.shape[0], w_in.shape[1]
    _, t_p, _ = x_prompt.shape
    nb, t_s, _ = x_sample.shape
    dff = w_ffn_down.shape[1]
    caches = (cache_dil_w128, cache_dil_w512, cache_dil_w2048)
    assert x_prompt.shape[0] == 1 and t_s <= SROWS // 2 and d == 2 * DIL_OUT
    for c, (win, _) in zip(caches, DIL_PAIRS):
        assert c.shape[2] == win and t_p >= win
    tail = max(w for w, _ in DIL_PAIRS)
    dils = tuple(dl for _, dl in DIL_PAIRS)
    tm = 256
    tm_big = min(512, t_p)
    m_s = nb * SROWS
    tm_s = min(tm, m_s)
    tm_ffn_s = min(128, m_s)

    xp = x_prompt.reshape(t_p, d)
    xs = jnp.pad(x_sample, ((0, 0), (0, SROWS - t_s), (0, 0))).reshape(nb * SROWS, d)
    cos_p, sin_p = _rope_tables(jnp.arange(t_p, dtype=F32))
    pos_s = PAST_LEN + jnp.minimum(jnp.arange(SROWS), t_s - 1).astype(F32)
    cos_s, sin_s = (jnp.tile(a, (nb, 1)) for a in _rope_tables(pos_s))
    cache_v = [c.reshape(depth, nb, c.shape[2] * 2 * DIL_HEADS, DIL_HD) for c in caches]
    new_cache = None

    p_gla, p_kv, p_conv, s_gla, s_conv = [], [], [], [], []
    row2 = lambda a: a.reshape(1, -1)
    for i in range(depth):
        wz, wal, wkv = _prep_in(w_in[i], d)
        wgu = jnp.pad(w_gate_up[i], ((0, LANES - GLA_RANK), (0, 0))).astype(BF16)
        bg = row2(b_gate[i])
        wa, wb, wo = (w.astype(BF16) for w in (w_branch_a[i], w_branch_b[i], w_out[i]))
        wup, wdn = w_ffn_up[i].astype(BF16), w_ffn_down[i].astype(BF16)
        n1, n2, n3, n4 = (row2(n[i]) for n in (norm_pre_mix, norm_post_mix, norm_pre_ffn, norm_post_ffn))
        gn = row2(gla_norm[i])
        wc = jnp.pad(w_conv[i], ((0, SUBLANES - CONV_W), (0, 0)))
        bc = row2(b_conv[i])

        z, la, *qkv = _inproj(xp, n1, wz, wal, wgu, bg, cos_p, sin_p, tm_big, dils)
        kvt = _kvtail(xp[t_p - tail:], n1, wkv, cos_p[t_p - tail:], sin_p[t_p - tail:], tm)
        oa, sfin = _gla_prompt(z, la)
        dil = [_dil_prompt(a, min(1024, a.shape[1])) for a in qkv]
        h = _merge(xp, oa, z, [o for o, _ in dil], [l for _, l in dil], gn, wa, wb, wo, n2, tm_big)
        roll = (t_s * 2 * DIL_HEADS, cache_v) if i == 0 else None
        xp, gst, rolled = _ffn(h, n3, wup, wc, bc, wdn, n4, None, tm_big, 512, roll=roll)
        if i == 0:
            new_cache = rolled
        p_gla.append(sfin[None])
        p_kv.append(kvt)
        p_conv.append(gst[-(CONV_W - 1):][None])

        z, la, *qkv = _inproj(xs, n1, wz, wal, wgu, bg, cos_s, sin_s, tm_s, (1,) * N_GROUPS)
        kvn = _kvtail(xs, n1, wkv, cos_s, sin_s, tm_s)
        oa, sfin = _gla_sample(z, la, state_gla[i], t_s)
        qf = jnp.concatenate([a[0, :, :CW] for a in qkv], axis=1).astype(F32)
        ob, new_cache = _dil_sample(qf, kvn, cache_v, new_cache, i, t_s, 1)
        h = _merge(xs, oa, z, ob, None, gn, wa, wb, wo, n2, tm_s)
        st = state_ffn_conv[i]
        zpad = lambda a: jnp.pad(a, ((0, 0), (0, SROWS - a.shape[1]), (0, 0))).reshape(m_s, dff)
        prev = (zpad(st[:, 1:2]), zpad(st))
        xs, gpre, _ = _ffn(h, n3, wup, wc, bc, wdn, n4, prev, tm_ffn_s, 512)
        s_gla.append(sfin)
        s_conv.append(gpre.reshape(nb, SROWS, dff)[:, t_s - (CONV_W - 1):t_s])

    y_p = xp.reshape(1, t_p, d)
    y_s = xs.reshape(nb, SROWS, d)[:, :t_s]
    p_caches = []
    for g, (win, _) in enumerate(DIL_PAIRS):
        rows = [kv[tail - win:, g * 2 * CW:(g + 1) * 2 * CW].reshape(1, win, 2, DIL_HEADS, DIL_HD) for kv in p_kv]
        p_caches.append(jnp.stack(rows))
    s_caches = [c.reshape(o.shape) for c, o in zip(new_cache, caches)]
    return (y_p, y_s, jnp.stack(p_gla), *p_caches, jnp.stack(p_conv),
            jnp.stack(s_gla), *s_caches, jnp.stack(s_conv))
```

```python
import functools

import numpy as np
import jax
import jax.numpy as jnp
from jax import lax
from jax.experimental import pallas as pl
from jax.experimental.pallas import tpu as pltpu

F32 = jnp.float32
BF16 = jnp.bfloat16

GLA_HEADS = 4
GLA_DK = 128
GLA_DV = 256
GLA_RANK = 16
GLA_TAU = 16.0
DIL_PAIRS = ((128, 1), (512, 4), (2048, 16))
N_GROUPS = len(DIL_PAIRS)
DIL_HEADS = 4
DIL_HD = 128
DIL_STEPS = 128
DIL_OUT = DIL_HEADS * DIL_HD
ROPE_THETA = 500000.0
ROPE_DIM = DIL_HD // 4
ROPE_HALF = ROPE_DIM // 2
CONV_W = 3
EPS = 1e-6
PAST_LEN = 16384

LANES = 128
SUBLANES = 8
VMEM_LIMIT = 56 * 1024 * 1024

CW = 512
Z_QA, Z_KA, Z_VA, Z_RA, Z_GA, Z_GB, Z_QB, Z_KB, Z_VB = 0, 1, 2, 4, 6, 8, 10, 13, 16
Z_BLOCKS = 19
SROWS = 8
DEC_T = 4
GLA_C = 128
NEG = -1e30
COPY_ROWS = 32


def _params(sem):
    return pltpu.CompilerParams(dimension_semantics=sem, vmem_limit_bytes=VMEM_LIMIT)


def _resident(shape):
    nd = len(shape)
    return pl.BlockSpec(shape, lambda *_: (0,) * nd, pipeline_mode=pl.Buffered(1))


def _rms(x, w):
    ms = jnp.mean(x * x, axis=-1, keepdims=True)
    return x * lax.rsqrt(ms + EPS) * w


def _dot(a, b):
    return jnp.dot(a, b, preferred_element_type=F32)


def _dot_nt(a, b):
    return lax.dot_general(a, b, (((1,), (1,)), ((), ())), preferred_element_type=F32)


def _rope_heads(acc, cos_t, sin_t, lt):
    parts = []
    for h in range(DIL_HEADS):
        a = acc[:, h * LANES:(h + 1) * LANES]
        up = pltpu.roll(a, LANES - ROPE_HALF, axis=1)
        dn = pltpu.roll(a, ROPE_HALF, axis=1)
        parts.append(a * cos_t + jnp.where(lt, up, dn) * sin_t)
    return parts


def _inproj_kernel(x_ref, nw_ref, wz_ref, wal_ref, wgu_ref, bg_ref, cos_ref, sin_ref,
                   z_ref, la_ref, g0_ref, g1_ref, g2_ref, scr, *, dils):
    tm = x_ref.shape[0]
    xn = _rms(x_ref[...], nw_ref[...]).astype(BF16)
    cos_t = cos_ref[...]
    sin_t = sin_ref[...]
    lt = lax.broadcasted_iota(jnp.int32, cos_t.shape, 1) < ROPE_HALF
    grefs = (g0_ref, g1_ref, g2_ref)
    for c in range(Z_BLOCKS):
        acc = _dot(xn, wz_ref[:, c * CW:(c + 1) * CW])
        if c < Z_QB:
            if c == Z_QA:
                acc = acc * (GLA_DK ** -0.5)
            elif c >= Z_GA:
                acc = jax.nn.sigmoid(acc)
            z_ref[:, c * CW:(c + 1) * CW] = acc.astype(BF16)
            continue
        part, g = divmod(c - Z_QB, N_GROUPS)
        if part < 2:
            acc = jnp.concatenate(_rope_heads(acc, cos_t, sin_t, lt), axis=1)
        if part == 0:
            acc = acc * (DIL_HD ** -0.5)
        cols = slice(part * CW, (part + 1) * CW)
        dil = dils[g]
        if dil == 1:
            grefs[g][0, :, cols] = acc.astype(BF16)
        else:
            for j in range(CW // LANES):
                scr[j] = acc[:, j * LANES:(j + 1) * LANES]
            for r in range(dil):
                rows = [scr[j, pl.ds(r, tm // dil, stride=dil), :] for j in range(CW // LANES)]
                grefs[g][r, :, cols] = jnp.concatenate(rows, axis=1).astype(BF16)
    al = _dot(xn, wal_ref[...]).astype(BF16)
    lg = _dot(al, wgu_ref[...]) + bg_ref[...]
    la_ref[...] = (jnp.minimum(lg, 0.0) - jnp.log1p(jnp.exp(-jnp.abs(lg)))) * (1.0 / GLA_TAU)


def _inproj(x, nw, wz, wal, wgu, bg, cos_t, sin_t, tm, dils):
    m, d = x.shape
    nw_all = wz.shape[1]
    nz = Z_QB * CW
    qk = wgu.shape[1]
    row = lambda w: pl.BlockSpec((tm, w), lambda i: (i, 0))
    gspec = lambda dl: pl.BlockSpec((dl, tm // dl, 3 * CW), lambda i: (0, i, 0))
    return pl.pallas_call(
        functools.partial(_inproj_kernel, dils=dils),
        grid=(m // tm,),
        in_specs=[row(d), _resident((1, d)), _resident((d, nw_all)), _resident((d, LANES)),
                  _resident((LANES, qk)), _resident((1, qk)), row(LANES), row(LANES)],
        out_specs=[row(nz), row(qk)] + [gspec(dl) for dl in dils],
        out_shape=[jax.ShapeDtypeStruct((m, nz), BF16), jax.ShapeDtypeStruct((m, qk), F32)]
        + [jax.ShapeDtypeStruct((dl, m // dl, 3 * CW), BF16) for dl in dils],
        scratch_shapes=[pltpu.VMEM((CW // LANES, tm, LANES), F32)],
        compiler_params=_params(("parallel",)),
        name="inproj",
    )(x, nw, wz, wal, wgu, bg, cos_t, sin_t)


def _kvtail_kernel(x_ref, nw_ref, w_ref, cos_ref, sin_ref, o_ref):
    xn = _rms(x_ref[...], nw_ref[...]).astype(BF16)
    cos_t = cos_ref[...]
    sin_t = sin_ref[...]
    lt = lax.broadcasted_iota(jnp.int32, cos_t.shape, 1) < ROPE_HALF
    for c in range(2 * N_GROUPS):
        acc = _dot(xn, w_ref[:, c * CW:(c + 1) * CW])
        if c % 2 == 0:
            for h, p in enumerate(_rope_heads(acc, cos_t, sin_t, lt)):
                o_ref[:, c * CW + h * LANES:c * CW + (h + 1) * LANES] = p
        else:
            o_ref[:, c * CW:(c + 1) * CW] = acc


def _kvtail(x, nw, wkv, cos_t, sin_t, tm):
    m, d = x.shape
    n = wkv.shape[1]
    row = lambda w: pl.BlockSpec((tm, w), lambda i: (i, 0))
    return pl.pallas_call(
        _kvtail_kernel,
        grid=(m // tm,),
        in_specs=[row(d), _resident((1, d)), _resident((d, n)), row(LANES), row(LANES)],
        out_specs=row(n),
        out_shape=jax.ShapeDtypeStruct((m, n), F32),
        compiler_params=_params(("parallel",)),
        name="kvtail",
    )(x, nw, wkv, cos_t, sin_t)


@functools.lru_cache(maxsize=None)
def _gla_consts(C, nl, seg):
    x = np.arange(C)[:, None]
    i = np.arange(C)[None, :]
    same = (x // seg == i // seg) if seg else np.ones((C, C), bool)
    mats = [(i <= x) & same, (i > x) & same]
    masks = [x == i]
    for l in range(nl):
        m = 1 << l
        r = (x | (2 * m - 1)) - m
        mats.append((i > np.minimum(x, r)) & (i <= np.maximum(x, r)))
        masks.append((((x >> l) ^ (i >> l)) == 1) & (((x >> l) & 1) == 1))
    pall = np.concatenate(mats, 0).astype(np.float32)
    return pall, np.stack(masks).astype(np.float32)


def _gla_core(q, k, v, g, pall_ref, mask_ref, nl):
    C = q.shape[0]
    g_hi = g.astype(BF16)
    g_lo = (g - g_hi.astype(F32)).astype(BF16)
    ex = _dot(pall_ref[...], jnp.concatenate([g_hi, g_lo], axis=1))
    ex = ex[:, :GLA_DK] + ex[:, GLA_DK:]
    b = ex[0:C]
    aft = ex[C:2 * C]
    qf = q.astype(F32)
    kf = k.astype(F32)
    a = _dot_nt(q, k) * mask_ref[0]
    for l in range(nl):
        e = jnp.exp(ex[(2 + l) * C:(3 + l) * C])
        a = a + _dot_nt((qf * e).astype(BF16), (kf * e).astype(BF16)) * mask_ref[1 + l]
    o = _dot(a.astype(BF16), v)
    qe = (qf * jnp.exp(b)).astype(BF16)
    kdt = (kf * jnp.exp(aft)).T.astype(BF16)
    return o, qe, kdt, b.T


def _gla_prompt_kernel(*refs, nl, nroll, aliased):
    q_ref, k_ref, v_ref, g_ref, pall_ref, mask_ref = refs[:6]
    roll_in = refs[6:6 + 2 * nroll]
    outs = refs[6 + (3 if aliased else 2) * nroll:]
    o_ref, sf_ref = outs[:2]
    roll_out = outs[2:2 + nroll]
    s_scr = outs[2 + nroll]
    i = pl.program_id(0)

    @pl.when(i == 0)
    def _():
        s_scr[...] = jnp.zeros_like(s_scr)

    for g in range(nroll):
        blk_ref, halo_ref = roll_in[2 * g], roll_in[2 * g + 1]
        rows, drop = blk_ref.shape[0], halo_ref.shape[0]
        roll_out[g][0:rows - drop, :] = blk_ref[drop:rows, :]
        roll_out[g][rows - drop:rows, :] = halo_ref[...]

    C = q_ref.shape[0]
    for h in range(GLA_HEADS):
        ks = slice(h * GLA_DK, (h + 1) * GLA_DK)
        vs = slice(h * GLA_DV, (h + 1) * GLA_DV)
        v = v_ref[:, vs]
        o, qe, kdt, bt = _gla_core(q_ref[:, ks], k_ref[:, ks], v, g_ref[:, ks], pall_ref, mask_ref, nl)
        s = s_scr[h]
        o_ref[:, vs] = (o + _dot(qe, s.astype(BF16))).astype(BF16)
        s_scr[h] = s * jnp.exp(bt[:, C - 1:C]) + _dot(kdt, v)

    @pl.when(i == pl.num_programs(0) - 1)
    def _():
        sf_ref[...] = s_scr[...]


def _gla_prompt(z, la, caches, rolled, layer, drop):
    m = z.shape[0]
    C = GLA_C
    nsteps = m // C
    nl = C.bit_length() - 1
    pall, masks = _gla_consts(C, nl, 0)
    pall = jnp.asarray(pall, BF16)
    masks = jnp.asarray(masks, F32)
    qk = GLA_HEADS * GLA_DK
    va = GLA_HEADS * GLA_DV
    aliased = rolled is not None
    nb = caches[0].shape[1]
    spw = nsteps // nb
    assert spw * nb == nsteps
    roll_in, roll_args, roll_out = [], [], []
    for c in caches:
        wr = c.shape[2]
        rows = wr // spw
        assert rows * spw == wr and rows % drop == 0
        per = rows // drop
        roll_in += [pl.BlockSpec((None, None, rows, LANES), lambda i: (layer, i // spw, i % spw, 0)),
                    pl.BlockSpec((None, None, drop, LANES),
                                 lambda i, per=per, last=wr // drop - 1:
                                 (layer, i // spw, jnp.minimum((i % spw + 1) * per, last), 0))]
        roll_args += [c, c]
        roll_out.append(pl.BlockSpec((None, None, rows, LANES), lambda i: (layer, i // spw, i % spw, 0)))
    anyspec = pl.BlockSpec(memory_space=pl.ANY)
    nroll = len(caches)
    n_in = 6 + 2 * nroll
    outs = pl.pallas_call(
        functools.partial(_gla_prompt_kernel, nl=nl, nroll=nroll, aliased=aliased),
        grid=(nsteps,),
        in_specs=[pl.BlockSpec((C, qk), lambda i: (i, Z_QA)),
                  pl.BlockSpec((C, qk), lambda i: (i, Z_KA)),
                  pl.BlockSpec((C, va), lambda i: (i, Z_VA * CW // va)),
                  pl.BlockSpec((C, qk), lambda i: (i, 0)),
                  _resident(pall.shape), _resident(masks.shape)] + roll_in
        + ([anyspec] * nroll if aliased else []),
        out_specs=[pl.BlockSpec((C, va), lambda i: (i, 0)),
                   pl.BlockSpec((GLA_HEADS, GLA_DK, GLA_DV), lambda i: (0, 0, 0))] + roll_out,
        out_shape=[jax.ShapeDtypeStruct((m, va), BF16),
                   jax.ShapeDtypeStruct((GLA_HEADS, GLA_DK, GLA_DV), F32)]
        + [jax.ShapeDtypeStruct(c.shape, F32) for c in caches],
        scratch_shapes=[pltpu.VMEM((GLA_HEADS, GLA_DK, GLA_DV), F32)],
        input_output_aliases={n_in + g: 2 + g for g in range(nroll)} if aliased else {},
        compiler_params=_params(("arbitrary",)),
        name="gla_prompt",
    )(z, z, z, la, pall, masks, *roll_args, *(rolled if aliased else []))
    return outs[0], outs[1], list(outs[2:])


def _gla_sample_kernel(q_ref, k_ref, v_ref, g_ref, s0_ref, pall_ref, mask_ref, o_ref, sf_ref, *, nl, tvalid):
    C = q_ref.shape[0]
    nseq = C // SROWS
    row = lax.broadcasted_iota(jnp.int32, (C, 1), 0)
    g = jnp.where(row % SROWS < tvalid, g_ref[...], 0.0)
    v = v_ref[...]
    o, qe, kdt, bt = _gla_core(q_ref[...], k_ref[...], v, g, pall_ref, mask_ref, nl)
    zero_v = jnp.zeros_like(v)
    for j in range(nseq):
        mine = row // SROWS == j
        s = s0_ref[j, 0]
        o = o + jnp.where(mine, _dot(qe, s.astype(BF16)), 0.0)
        last = j * SROWS + SROWS - 1
        sf_ref[j, 0] = s * jnp.exp(bt[:, last:last + 1]) + _dot(kdt, jnp.where(mine, v, zero_v))
    o_ref[...] = o.astype(BF16)


def _gla_sample(z, la, s0, tvalid):
    m = z.shape[0]
    C = GLA_C
    nseq = C // SROWS
    nl = (tvalid - 1).bit_length()
    pall, masks = _gla_consts(C, nl, SROWS)
    pall = jnp.asarray(pall, BF16)
    masks = jnp.asarray(masks, F32)
    va = GLA_HEADS * GLA_DV
    sblk = (nseq, 1, GLA_DK, GLA_DV)
    return pl.pallas_call(
        functools.partial(_gla_sample_kernel, nl=nl, tvalid=tvalid),
        grid=(m // C, GLA_HEADS),
        in_specs=[pl.BlockSpec((C, GLA_DK), lambda c, h: (c, Z_QA * CW // GLA_DK + h)),
                  pl.BlockSpec((C, GLA_DK), lambda c, h: (c, Z_KA * CW // GLA_DK + h)),
                  pl.BlockSpec((C, GLA_DV), lambda c, h: (c, Z_VA * CW // GLA_DV + h)),
                  pl.BlockSpec((C, GLA_DK), lambda c, h: (c, h)),
                  pl.BlockSpec(sblk, lambda c, h: (c, h, 0, 0)),
                  _resident(pall.shape), _resident(masks.shape)],
        out_specs=[pl.BlockSpec((C, GLA_DV), lambda c, h: (c, h)),
                   pl.BlockSpec(sblk, lambda c, h: (c, h, 0, 0))],
        out_shape=[jax.ShapeDtypeStruct((m, va), BF16),
                   jax.ShapeDtypeStruct(s0.shape, F32)],
        compiler_params=_params(("parallel", "parallel")),
        name="gla_sample",
    )(z, z, z, la, s0, pall, masks)


def _dil_prompt_kernel(q_ref, k_ref, kp_ref, v_ref, vp_ref, o_ref, lse_ref):
    i = pl.program_id(1)
    rb = q_ref.shape[0]
    nk = 2 * DIL_STEPS
    qi = lax.broadcasted_iota(jnp.int32, (DIL_STEPS, nk), 0)
    ki = lax.broadcasted_iota(jnp.int32, (DIL_STEPS, nk), 1)
    lane = lax.broadcasted_iota(jnp.int32, (DIL_STEPS, LANES), 1)
    band = (ki >= qi) & (ki <= qi + DIL_STEPS)
    bias = jnp.where(band, 0.0, NEG)
    bias_first = jnp.where((i > 0) | (ki >= DIL_STEPS), bias, NEG)
    ones = jnp.ones((nk, DIL_HD), BF16)
    for sb in range(rb // DIL_STEPS):
        rows = slice(sb * DIL_STEPS, (sb + 1) * DIL_STEPS)
        lse_tile = jnp.zeros((DIL_STEPS, LANES), F32)
        for h in range(DIL_HEADS):
            hs = slice(h * DIL_HD, (h + 1) * DIL_HD)
            if sb == 0:
                kcat = jnp.concatenate([kp_ref[:, hs], k_ref[rows, hs]], axis=0)
                vcat = jnp.concatenate([vp_ref[:, hs], v_ref[rows, hs]], axis=0)
            else:
                krows = slice((sb - 1) * DIL_STEPS, (sb + 1) * DIL_STEPS)
                kcat, vcat = k_ref[krows, hs], v_ref[krows, hs]
            s = _dot_nt(q_ref[rows, hs], kcat) + (bias_first if sb == 0 else bias)
            mx = jnp.max(s, axis=-1, keepdims=True)
            p = jnp.exp(s - mx).astype(BF16)
            pv = _dot(p, jnp.concatenate([vcat, ones], axis=1))
            den = pv[:, DIL_HD:]
            o_ref[rows, hs] = (pv[:, :DIL_HD] / den).astype(BF16)
            lse_tile = jnp.where(lane == h, mx + jnp.log(den), lse_tile)
        lse_ref[rows, :] = lse_tile


def _dil_prompt(qkv, rb):
    dil, l, _ = qkv.shape
    nprev = rb // DIL_STEPS
    own = lambda c: pl.BlockSpec((None, rb, CW), lambda r, i: (r, i, c))
    prev = lambda c: pl.BlockSpec((None, DIL_STEPS, CW), lambda r, i: (r, jnp.maximum(i * nprev - 1, 0), c))
    return pl.pallas_call(
        _dil_prompt_kernel,
        grid=(dil, l // rb),
        in_specs=[own(0), own(1), prev(1), own(2), prev(2)],
        out_specs=[pl.BlockSpec((None, rb, CW), lambda r, i: (r, i, 0)),
                   pl.BlockSpec((None, rb, LANES), lambda r, i: (r, i, 0))],
        out_shape=[jax.ShapeDtypeStruct((dil, l, CW), BF16),
                   jax.ShapeDtypeStruct((dil, l, LANES), F32)],
        compiler_params=_params(("parallel", "parallel")),
        name=f"dil_prompt_d{dil}",
    )(qkv, qkv, qkv, qkv, qkv)


NUNIT = DEC_T * DIL_HEADS
KPAD = 2 * DIL_STEPS


def _dil_sample_kernel(q_ref, kvn_ref, c0_ref, c1_ref, c2_ref, a0_ref, a1_ref, a2_ref,
                       o_ref, n0_ref, n1_ref, n2_ref, *, bb, tvalid):
    del a0_ref, a1_ref, a2_ref
    rpt = 2 * DIL_HEADS
    qall = q_ref[...]
    kvn = kvn_ref[...]
    row_u = lax.broadcasted_iota(jnp.int32, (NUNIT, KPAD), 0)
    key = lax.broadcasted_iota(jnp.int32, (NUNIT, KPAD), 1)
    t_u = row_u // DIL_HEADS
    newt = key - DIL_STEPS
    bias_d1 = jnp.where(key < DIL_STEPS, jnp.where(key >= t_u, 0.0, NEG), jnp.where(newt <= t_u, 0.0, NEG))
    bias_dn = jnp.where(key < DIL_STEPS, 0.0, jnp.where(newt == t_u, 0.0, NEG))
    blk_of_lane = lax.broadcasted_iota(jnp.int32, (NUNIT, NUNIT * DIL_HD), 1) // DIL_HD
    diag = lax.broadcasted_iota(jnp.int32, (NUNIT, NUNIT * DIL_HD), 0) == blk_of_lane
    row16 = lax.broadcasted_iota(jnp.int32, (NUNIT, DIL_HD), 0)
    zpad = jnp.zeros((KPAD - DIL_STEPS - 2 * SROWS, NUNIT * DIL_HD), BF16)
    zrow = jnp.zeros((SROWS, NUNIT * DIL_HD), F32)
    c_refs = (c0_ref, c1_ref, c2_ref)
    n_refs = (n0_ref, n1_ref, n2_ref)

    for b in range(bb):
        rows8 = slice(b * SROWS, (b + 1) * SROWS)
        outs, lses = [], []
        for g, (_, dil) in enumerate(DIL_PAIRS):
            c_ref = c_refs[g]
            knew = kvn[rows8, g * 2 * CW:g * 2 * CW + CW]
            vnew = kvn[rows8, g * 2 * CW + CW:(g + 1) * 2 * CW]
            for t in range(tvalid):
                r = b * SROWS + t
                n_refs[g][b, t * rpt:(t + 1) * rpt, :] = jnp.concatenate(
                    [kvn[r:r + 1, g * 2 * CW + j * DIL_HD:g * 2 * CW + (j + 1) * DIL_HD] for j in range(rpt)],
                    axis=0)
            q16 = [jnp.concatenate(
                [jnp.broadcast_to(qall[b * SROWS + t:b * SROWS + t + 1, g * CW + h * DIL_HD:g * CW + (h + 1) * DIL_HD],
                                  (DIL_HEADS, DIL_HD)) for t in range(DEC_T)], axis=0) for h in range(DIL_HEADS)]
            kb, vb, qb = {}, {}, []
            for u in range(NUNIT):
                t, h = divmod(u, DIL_HEADS)
                first = (t if dil > 1 else 0) * rpt + h
                if first not in kb:
                    kb[first] = c_ref[b, pl.ds(first, DIL_STEPS, stride=rpt * dil), :].astype(BF16)
                    vb[first] = c_ref[b, pl.ds(first + DIL_HEADS, DIL_STEPS, stride=rpt * dil), :].astype(BF16)
                qb.append(jnp.where(row16 == u, q16[h], 0.0))
            firsts = [(divmod(u, DIL_HEADS)[0] if dil > 1 else 0) * rpt + u % DIL_HEADS for u in range(NUNIT)]
            hcols = lambda a: jnp.concatenate([a[:, (u % DIL_HEADS) * DIL_HD:(u % DIL_HEADS + 1) * DIL_HD]
                                               for u in range(NUNIT)], axis=1)
            k_ext = jnp.concatenate([jnp.concatenate([kb[f] for f in firsts], axis=1),
                                     jnp.concatenate([hcols(knew), zrow], axis=0).astype(BF16), zpad], axis=0)
            v_ext = jnp.concatenate([jnp.concatenate([vb[f] for f in firsts], axis=1),
                                     jnp.concatenate([hcols(vnew), zrow], axis=0).astype(BF16), zpad], axis=0)
            qbd = jnp.concatenate(qb, axis=1).astype(BF16)
            s = _dot_nt(qbd, k_ext) + (bias_d1 if dil == 1 else bias_dn)
            mx = jnp.max(s, axis=-1, keepdims=True)
            p = jnp.exp(s - mx)
            den = jnp.sum(p, axis=-1, keepdims=True)
            outs.append(_dot(p.astype(BF16), v_ext) / den)
            lses.append(mx + jnp.log(den))
        mxl = jnp.maximum(jnp.maximum(lses[0], lses[1]), lses[2])
        es = [jnp.exp(l - mxl) for l in lses]
        inv = 1.0 / (es[0] + es[1] + es[2])
        merged = (es[0] * inv) * outs[0] + (es[1] * inv) * outs[1] + (es[2] * inv) * outs[2]
        red = jnp.sum(jnp.where(diag, merged, 0.0), axis=0, keepdims=True)
        orow = [red[:, t * DIL_OUT:(t + 1) * DIL_OUT] for t in range(DEC_T)]
        o_ref[rows8, :] = jnp.concatenate(orow + [jnp.zeros((SROWS - DEC_T, DIL_OUT), F32)], axis=0)


def _dil_sample(qf, kvn, caches, rolled, layer, tvalid, bb):
    assert tvalid == DEC_T
    m = qf.shape[0]
    nb = caches[0].shape[1]
    rows = bb * SROWS
    rpt = 2 * DIL_HEADS
    cspecs, nspecs = [], []
    for c in caches:
        wr = c.shape[2]
        cspecs.append(pl.BlockSpec((None, bb, wr, LANES), lambda i: (layer, i, 0, 0)))
        nspecs.append(pl.BlockSpec((None, bb, tvalid * rpt, LANES),
                                   lambda i, last=wr // (tvalid * rpt) - 1: (layer, i, last, 0)))
    anyspec = pl.BlockSpec(memory_space=pl.ANY)
    outs = pl.pallas_call(
        functools.partial(_dil_sample_kernel, bb=bb, tvalid=tvalid),
        grid=(nb // bb,),
        in_specs=[pl.BlockSpec((rows, N_GROUPS * CW), lambda i: (i, 0)),
                  pl.BlockSpec((rows, N_GROUPS * 2 * CW), lambda i: (i, 0))] + cspecs + [anyspec] * N_GROUPS,
        out_specs=[pl.BlockSpec((rows, CW), lambda i: (i, 0))] + nspecs,
        out_shape=[jax.ShapeDtypeStruct((m, CW), F32)] + [jax.ShapeDtypeStruct(r.shape, F32) for r in rolled],
        input_output_aliases={2 + N_GROUPS + g: 1 + g for g in range(N_GROUPS)},
        compiler_params=_params(("parallel",)),
        name="dil_sample",
    )(qf, kvn, *caches, *rolled)
    return outs[0], list(outs[1:])


def _tokens_from_streams(ref, scr):
    dil, n, w = ref.shape
    if dil == 1:
        return ref[0].astype(F32)
    nslab = w // LANES
    for r in range(dil):
        blk = ref[r].astype(F32)
        for j in range(nslab):
            scr[j, pl.ds(r, n, stride=dil), :] = blk[:, j * LANES:(j + 1) * LANES]
    return jnp.concatenate([scr[j] for j in range(nslab)], axis=1)


def _merge_kernel(*refs, grouped):
    if grouped:
        (x_ref, oa_ref, r_ref, ga_ref, gb_ref, o0_ref, o1_ref, o2_ref, l0_ref, l1_ref, l2_ref,
         gn_ref, wa_ref, wb_ref, wo_ref, nw_ref, h_ref, o_scr, l_scr) = refs
    else:
        x_ref, oa_ref, r_ref, ga_ref, gb_ref, ob_ref, gn_ref, wa_ref, wb_ref, wo_ref, nw_ref, h_ref = refs
    oa = oa_ref[...].astype(F32)
    gn = gn_ref[...]
    parts = [_rms(oa[:, h * GLA_DV:(h + 1) * GLA_DV], gn) for h in range(GLA_HEADS)]
    r = r_ref[...].astype(F32)
    oan = (jnp.concatenate(parts, axis=1) * (r * jax.nn.sigmoid(r))).astype(BF16)

    if grouped:
        lses = [_tokens_from_streams(l, l_scr) for l in (l0_ref, l1_ref, l2_ref)]
        mx = jnp.maximum(jnp.maximum(lses[0], lses[1]), lses[2])
        es = [jnp.exp(l - mx) for l in lses]
        inv = 1.0 / (es[0] + es[1] + es[2])
        outs = [_tokens_from_streams(o, o_scr) for o in (o0_ref, o1_ref, o2_ref)]
        ob = []
        for h in range(DIL_HEADS):
            hs = slice(h * DIL_HD, (h + 1) * DIL_HD)
            acc = None
            for e, o in zip(es, outs):
                term = (e * inv)[:, h:h + 1] * o[:, hs]
                acc = term if acc is None else acc + term
            ob.append(acc)
        ob = jnp.concatenate(ob, axis=1).astype(BF16)
    else:
        ob = ob_ref[...].astype(BF16)

    merged = (ga_ref[...].astype(F32) * _dot(oan, wa_ref[...])
              + gb_ref[...].astype(F32) * _dot(ob, wb_ref[...]))
    y = _dot(merged.astype(BF16), wo_ref[...])
    h_ref[...] = x_ref[...] + _rms(y, nw_ref[...])


def _merge(x, oa, z, dil_o, dil_lse, gn, wa, wb, wo, nw, tm):
    m, d = x.shape
    va = oa.shape[1]
    grouped = dil_lse is not None
    zc = lambda c: pl.BlockSpec((tm, d), lambda i: (i, c * CW // d))
    row = lambda w: pl.BlockSpec((tm, w), lambda i: (i, 0))
    stream = lambda a: pl.BlockSpec((a.shape[0], tm // a.shape[0], a.shape[2]), lambda i: (0, i, 0))
    if grouped:
        dil_specs = [stream(a) for a in dil_o] + [stream(a) for a in dil_lse]
        dil_args = [*dil_o, *dil_lse]
        scratch = [pltpu.VMEM((CW // LANES, tm, LANES), F32), pltpu.VMEM((1, tm, LANES), F32)]
    else:
        dil_specs, dil_args, scratch = [row(CW)], [dil_o], []
    return pl.pallas_call(
        functools.partial(_merge_kernel, grouped=grouped),
        grid=(m // tm,),
        in_specs=[row(d), row(va), zc(Z_RA), zc(Z_GA), zc(Z_GB)] + dil_specs
        + [_resident(gn.shape), _resident(wa.shape), _resident(wb.shape),
           _resident(wo.shape), _resident(nw.shape)],
        out_specs=row(d),
        out_shape=jax.ShapeDtypeStruct((m, d), F32),
        scratch_shapes=scratch,
        compiler_params=_params(("parallel",)),
        name="merge",
    )(x, oa, z, z, z, *dil_args, gn, wa, wb, wo, nw)


def _gelu_tanh(x):
    return 0.5 * x * (1.0 + jnp.tanh(0.7978845608028654 * (x + 0.044715 * x * x * x)))


def _ffn_kernel(*refs, fc, sample):
    if sample:
        h_ref, nw1_ref, wup_ref, wc_ref, bc_ref, wdn_ref, nw2_ref, p1_ref, p2_ref, out_ref, gst_ref = refs
    else:
        h_ref, nw1_ref, wup_ref, wc_ref, bc_ref, wdn_ref, nw2_ref, out_ref, gst_ref, cbuf = refs
    tm = h_ref.shape[0]
    dff = wdn_ref.shape[0]
    i = pl.program_id(0)

    if not sample:
        @pl.when(i == 0)
        def _():
            cbuf[...] = jnp.zeros_like(cbuf)

    h = h_ref[...]
    hn = _rms(h, nw1_ref[...]).astype(BF16)
    if sample:
        tpos = lax.broadcasted_iota(jnp.int32, (tm, 1), 0) % SROWS
    else:
        row8 = lax.broadcasted_iota(jnp.int32, (SUBLANES, 1), 0)
    acc = None
    for c in range(dff // fc):
        cs = slice(c * fc, (c + 1) * fc)
        gp = _dot(hn, wup_ref[:, cs])
        val = _dot(hn, wup_ref[:, dff + c * fc:dff + (c + 1) * fc])
        g1 = pltpu.roll(gp, 1, axis=0)
        g2 = pltpu.roll(gp, 2, axis=0)
        if sample:
            g1 = jnp.where(tpos == 0, p1_ref[:, cs], g1)
            g2 = jnp.where(tpos < 2, p2_ref[:, cs], g2)
            gst_ref[:, cs] = gp
        else:
            carry = cbuf[:, cs]
            head1 = jnp.where(row8 < 1, pltpu.roll(carry, 1, axis=0), g1[0:SUBLANES])
            head2 = jnp.where(row8 < 2, pltpu.roll(carry, 2, axis=0), g2[0:SUBLANES])
            g1 = jnp.concatenate([head1, g1[SUBLANES:]], axis=0)
            g2 = jnp.concatenate([head2, g2[SUBLANES:]], axis=0)
            cbuf[:, cs] = gp[tm - SUBLANES:]
        conv = g2 * wc_ref[0:1, cs] + g1 * wc_ref[1:2, cs] + gp * wc_ref[2:3, cs] + bc_ref[:, cs]
        act = (_gelu_tanh(conv) * val).astype(BF16)
        part = _dot(act, wdn_ref[cs, :])
        acc = part if acc is None else acc + part
    if not sample:
        gst_ref[...] = cbuf[...]
    out_ref[...] = h + _rms(acc, nw2_ref[...])


def _ffn(h, nw1, wup, wc, bc, wdn, nw2, prev, tm, fc):
    m, d = h.shape
    dff = wdn.shape[0]
    sample = prev is not None
    row = lambda w: pl.BlockSpec((tm, w), lambda i: (i, 0))
    in_specs = [row(d), _resident((1, d)), _resident(wup.shape), _resident(wc.shape),
                _resident((1, dff)), _resident(wdn.shape), _resident((1, d))]
    args = [h, nw1, wup, wc, bc, wdn, nw2]
    out_specs, out_shape, scratch = [row(d)], [jax.ShapeDtypeStruct((m, d), F32)], []
    if sample:
        in_specs += [row(dff), row(dff)]
        args += list(prev)
        out_specs.append(row(dff))
        out_shape.append(jax.ShapeDtypeStruct((m, dff), F32))
    else:
        out_specs.append(pl.BlockSpec((SUBLANES, dff), lambda i: (i, 0)))
        out_shape.append(jax.ShapeDtypeStruct((m // tm * SUBLANES, dff), F32))
        scratch.append(pltpu.VMEM((SUBLANES, dff), F32))
    return pl.pallas_call(
        functools.partial(_ffn_kernel, fc=fc, sample=sample),
        grid=(m // tm,),
        in_specs=in_specs,
        out_specs=out_specs,
        out_shape=out_shape,
        scratch_shapes=scratch,
        compiler_params=_params(("arbitrary",)),
        name="ffn_sample" if sample else "ffn_prompt",
    )(*args)


def _rope_tables(pos):
    inv = ROPE_THETA ** (-jnp.arange(ROPE_HALF, dtype=F32) * 2.0 / ROPE_DIM)
    ang = pos[:, None] * inv[None, :]
    cos, sin = jnp.cos(ang), jnp.sin(ang)
    pad = LANES - ROPE_DIM
    one = jnp.ones((pos.shape[0], pad), F32)
    cos_t = jnp.concatenate([cos, cos, one], axis=1)
    sin_t = jnp.concatenate([-sin, sin, 0.0 * one], axis=1)
    return cos_t, sin_t


def _prep_in(w_in_l, d):
    qk = GLA_HEADS * GLA_DK
    va = GLA_HEADS * GLA_DV
    nb = N_GROUPS * DIL_OUT
    sizes = (qk, qk, va, va, GLA_RANK, 3 * nb, 2 * d)
    offs = np.cumsum((0,) + sizes)
    q_a, k_a, v_a, r_a, a_low, qkv_b, gate = (w_in_l[:, offs[j]:offs[j + 1]] for j in range(7))
    q_b, k_b, v_b = qkv_b[:, :nb], qkv_b[:, nb:2 * nb], qkv_b[:, 2 * nb:]
    wz = jnp.concatenate([q_a, k_a, v_a, r_a, gate, q_b, k_b, v_b], axis=1).astype(BF16)
    wal = jnp.pad(a_low, ((0, 0), (0, LANES - GLA_RANK))).astype(BF16)
    kv = []
    for g in range(N_GROUPS):
        kv += [k_b[:, g * DIL_OUT:(g + 1) * DIL_OUT], v_b[:, g * DIL_OUT:(g + 1) * DIL_OUT]]
    wkv = jnp.concatenate(kv, axis=1).astype(BF16)
    return wz, wal, wkv


def kernel(x_prompt, x_sample, state_gla, cache_dil_w128, cache_dil_w512, cache_dil_w2048, state_ffn_conv,
           w_in, w_gate_up, b_gate, gla_norm, w_branch_a, w_branch_b, w_out, norm_pre_mix, norm_post_mix,
           norm_pre_ffn, norm_post_ffn, w_ffn_up, w_conv, b_conv, w_ffn_down):
    depth, d = w_in.shape[0], w_in.shape[1]
    _, t_p, _ = x_prompt.shape
    nb, t_s, _ = x_sample.shape
    dff = w_ffn_down.shape[1]
    caches = (cache_dil_w128, cache_dil_w512, cache_dil_w2048)
    assert x_prompt.shape[0] == 1 and t_s <= SROWS // 2 and d == 2 * DIL_OUT
    for c, (win, _) in zip(caches, DIL_PAIRS):
        assert c.shape[2] == win and t_p >= win
    tail = max(w for w, _ in DIL_PAIRS)
    dils = tuple(dl for _, dl in DIL_PAIRS)
    tm = 256
    tm_big = min(512, t_p)
    m_s = nb * SROWS
    tm_s = min(tm, m_s)
    tm_ffn_s = min(128, m_s)

    xp = x_prompt.reshape(t_p, d)
    xs = jnp.pad(x_sample, ((0, 0), (0, SROWS - t_s), (0, 0))).reshape(nb * SROWS, d)
    cos_p, sin_p = _rope_tables(jnp.arange(t_p, dtype=F32))
    pos_s = PAST_LEN + jnp.minimum(jnp.arange(SROWS), t_s - 1).astype(F32)
    cos_s, sin_s = (jnp.tile(a, (nb, 1)) for a in _rope_tables(pos_s))
    cache_v = [c.reshape(depth, nb, c.shape[2] * 2 * DIL_HEADS, DIL_HD) for c in caches]
    new_cache = None

    p_gla, p_kv, p_conv, s_gla, s_conv = [], [], [], [], []
    row2 = lambda a: a.reshape(1, -1)
    for i in range(depth):
        wz, wal, wkv = _prep_in(w_in[i], d)
        wgu = jnp.pad(w_gate_up[i], ((0, LANES - GLA_RANK), (0, 0))).astype(BF16)
        bg = row2(b_gate[i])
        wa, wb, wo = (w.astype(BF16) for w in (w_branch_a[i], w_branch_b[i], w_out[i]))
        wup, wdn = w_ffn_up[i].astype(BF16), w_ffn_down[i].astype(BF16)
        n1, n2, n3, n4 = (row2(n[i]) for n in (norm_pre_mix, norm_post_mix, norm_pre_ffn, norm_post_ffn))
        gn = row2(gla_norm[i])
        wc = jnp.pad(w_conv[i], ((0, SUBLANES - CONV_W), (0, 0)))
        bc = row2(b_conv[i])

        z, la, *qkv = _inproj(xp, n1, wz, wal, wgu, bg, cos_p, sin_p, tm_big, dils)
        kvt = _kvtail(xp[t_p - tail:], n1, wkv, cos_p[t_p - tail:], sin_p[t_p - tail:], tm)
        oa, sfin, new_cache = _gla_prompt(z, la, cache_v, new_cache, i, t_s * 2 * DIL_HEADS)
        dil = [_dil_prompt(a, min(1024, a.shape[1])) for a in qkv]
        h = _merge(xp, oa, z, [o for o, _ in dil], [l for _, l in dil], gn, wa, wb, wo, n2, tm_big)
        xp, gst = _ffn(h, n3, wup, wc, bc, wdn, n4, None, tm_big, 512)
        p_gla.append(sfin[None])
        p_kv.append(kvt)
        p_conv.append(gst[-(CONV_W - 1):][None])

        z, la, *qkv = _inproj(xs, n1, wz, wal, wgu, bg, cos_s, sin_s, tm_s, (1,) * N_GROUPS)
        kvn = _kvtail(xs, n1, wkv, cos_s, sin_s, tm_s)
        oa, sfin = _gla_sample(z, la, state_gla[i], t_s)
        qf = jnp.concatenate([a[0, :, :CW] for a in qkv], axis=1).astype(F32)
        ob, new_cache = _dil_sample(qf, kvn, cache_v, new_cache, i, t_s, 1)
        h = _merge(xs, oa, z, ob, None, gn, wa, wb, wo, n2, tm_s)
        st = state_ffn_conv[i]
        zpad = lambda a: jnp.pad(a, ((0, 0), (0, SROWS - a.shape[1]), (0, 0))).reshape(m_s, dff)
        prev = (zpad(st[:, 1:2]), zpad(st))
        xs, gpre = _ffn(h, n3, wup, wc, bc, wdn, n4, prev, tm_ffn_s, 512)
        s_gla.append(sfin)
        s_conv.append(gpre.reshape(nb, SROWS, dff)[:, t_s - (CONV_W - 1):t_s])

    y_p = xp.reshape(1, t_p, d)
    y_s = xs.reshape(nb, SROWS, d)[:, :t_s]
    p_caches = []
    for g, (win, _) in enumerate(DIL_PAIRS):
        rows = [kv[tail - win:, g * 2 * CW:(g + 1) * 2 * CW].reshape(1, win, 2, DIL_HEADS, DIL_HD) for kv in p_kv]
        p_caches.append(jnp.stack(rows))
    s_caches = [c.reshape(o.shape) for c, o in zip(new_cache, caches)]
    return (y_p, y_s, jnp.stack(p_gla), *p_caches, jnp.stack(p_conv),
            jnp.stack(s_gla), *s_caches, jnp.stack(s_conv))
```

```python
import functools

import numpy as np
import jax
import jax.numpy as jnp
from jax import lax
from jax.experimental import pallas as pl
from jax.experimental.pallas import tpu as pltpu

F32 = jnp.float32
BF16 = jnp.bfloat16

GLA_HEADS = 4
GLA_DK = 128
GLA_DV = 256
GLA_RANK = 16
GLA_TAU = 16.0
DIL_PAIRS = ((128, 1), (512, 4), (2048, 16))
N_GROUPS = len(DIL_PAIRS)
DIL_HEADS = 4
DIL_HD = 128
DIL_STEPS = 128
DIL_OUT = DIL_HEADS * DIL_HD
ROPE_THETA = 500000.0
ROPE_DIM = DIL_HD // 4
ROPE_HALF = ROPE_DIM // 2
CONV_W = 3
EPS = 1e-6
PAST_LEN = 16384

LANES = 128
SUBLANES = 8
VMEM_LIMIT = 56 * 1024 * 1024

CW = 512
Z_QA, Z_KA, Z_VA, Z_RA, Z_GA, Z_GB, Z_QB, Z_KB, Z_VB = 0, 1, 2, 4, 6, 8, 10, 13, 16
Z_BLOCKS = 19
SROWS = 8
DEC_T = 4
GLA_C = 128
NEG = -1e30
COPY_ROWS = 32


def _params(sem):
    return pltpu.CompilerParams(dimension_semantics=sem, vmem_limit_bytes=VMEM_LIMIT)


def _resident(shape):
    nd = len(shape)
    return pl.BlockSpec(shape, lambda *_: (0,) * nd, pipeline_mode=pl.Buffered(1))


def _rms(x, w):
    ms = jnp.mean(x * x, axis=-1, keepdims=True)
    return x * lax.rsqrt(ms + EPS) * w


def _dot(a, b):
    return jnp.dot(a, b, preferred_element_type=F32)


def _dot_nt(a, b):
    return lax.dot_general(a, b, (((1,), (1,)), ((), ())), preferred_element_type=F32)


def _rope_heads(acc, cos_t, sin_t, lt):
    parts = []
    for h in range(DIL_HEADS):
        a = acc[:, h * LANES:(h + 1) * LANES]
        up = pltpu.roll(a, LANES - ROPE_HALF, axis=1)
        dn = pltpu.roll(a, ROPE_HALF, axis=1)
        parts.append(a * cos_t + jnp.where(lt, up, dn) * sin_t)
    return parts


def _inproj_kernel(x_ref, nw_ref, wz_ref, wal_ref, wgu_ref, bg_ref, cos_ref, sin_ref,
                   z_ref, la_ref, g0_ref, g1_ref, g2_ref, scr, *, dils):
    tm = x_ref.shape[0]
    xn = _rms(x_ref[...], nw_ref[...]).astype(BF16)
    cos_t = cos_ref[...]
    sin_t = sin_ref[...]
    lt = lax.broadcasted_iota(jnp.int32, cos_t.shape, 1) < ROPE_HALF
    grefs = (g0_ref, g1_ref, g2_ref)
    for c in range(Z_BLOCKS):
        acc = _dot(xn, wz_ref[:, c * CW:(c + 1) * CW])
        if c < Z_QB:
            if c == Z_QA:
                acc = acc * (GLA_DK ** -0.5)
            elif c >= Z_GA:
                acc = jax.nn.sigmoid(acc)
            z_ref[:, c * CW:(c + 1) * CW] = acc.astype(BF16)
            continue
        part, g = divmod(c - Z_QB, N_GROUPS)
        if part < 2:
            acc = jnp.concatenate(_rope_heads(acc, cos_t, sin_t, lt), axis=1)
        if part == 0:
            acc = acc * (DIL_HD ** -0.5)
        cols = slice(part * CW, (part + 1) * CW)
        dil = dils[g]
        if dil == 1:
            grefs[g][0, :, cols] = acc.astype(BF16)
        else:
            for j in range(CW // LANES):
                scr[j] = acc[:, j * LANES:(j + 1) * LANES]
            for r in range(dil):
                rows = [scr[j, pl.ds(r, tm // dil, stride=dil), :] for j in range(CW // LANES)]
                grefs[g][r, :, cols] = jnp.concatenate(rows, axis=1).astype(BF16)
    al = _dot(xn, wal_ref[...]).astype(BF16)
    lg = _dot(al, wgu_ref[...]) + bg_ref[...]
    la_ref[...] = (jnp.minimum(lg, 0.0) - jnp.log1p(jnp.exp(-jnp.abs(lg)))) * (1.0 / GLA_TAU)


def _inproj(x, nw, wz, wal, wgu, bg, cos_t, sin_t, tm, dils):
    m, d = x.shape
    nw_all = wz.shape[1]
    nz = Z_QB * CW
    qk = wgu.shape[1]
    row = lambda w: pl.BlockSpec((tm, w), lambda i: (i, 0))
    gspec = lambda dl: pl.BlockSpec((dl, tm // dl, 3 * CW), lambda i: (0, i, 0))
    return pl.pallas_call(
        functools.partial(_inproj_kernel, dils=dils),
        grid=(m // tm,),
        in_specs=[row(d), _resident((1, d)), _resident((d, nw_all)), _resident((d, LANES)),
                  _resident((LANES, qk)), _resident((1, qk)), row(LANES), row(LANES)],
        out_specs=[row(nz), row(qk)] + [gspec(dl) for dl in dils],
        out_shape=[jax.ShapeDtypeStruct((m, nz), BF16), jax.ShapeDtypeStruct((m, qk), F32)]
        + [jax.ShapeDtypeStruct((dl, m // dl, 3 * CW), BF16) for dl in dils],
        scratch_shapes=[pltpu.VMEM((CW // LANES, tm, LANES), F32)],
        compiler_params=_params(("parallel",)),
        name="inproj",
    )(x, nw, wz, wal, wgu, bg, cos_t, sin_t)


def _kvtail_kernel(x_ref, nw_ref, w_ref, cos_ref, sin_ref, o_ref):
    xn = _rms(x_ref[...], nw_ref[...]).astype(BF16)
    cos_t = cos_ref[...]
    sin_t = sin_ref[...]
    lt = lax.broadcasted_iota(jnp.int32, cos_t.shape, 1) < ROPE_HALF
    for c in range(2 * N_GROUPS):
        acc = _dot(xn, w_ref[:, c * CW:(c + 1) * CW])
        if c % 2 == 0:
            for h, p in enumerate(_rope_heads(acc, cos_t, sin_t, lt)):
                o_ref[:, c * CW + h * LANES:c * CW + (h + 1) * LANES] = p
        else:
            o_ref[:, c * CW:(c + 1) * CW] = acc


def _kvtail_prompt_kernel(*refs, wins, aliased):
    x_ref, nw_ref, w_ref, cos_ref, sin_ref = refs[:5]
    o_refs = refs[5 + (len(wins) if aliased else 0):]
    tm = x_ref.shape[0]
    i = pl.program_id(0)
    ntile = pl.num_programs(0)
    xn = _rms(x_ref[...], nw_ref[...]).astype(BF16)
    cos_t = cos_ref[...]
    sin_t = sin_ref[...]
    lt = lax.broadcasted_iota(jnp.int32, cos_t.shape, 1) < ROPE_HALF
    rpt = 2 * DIL_HEADS
    for g, win in enumerate(wins):
        take = min(win, tm)

        @pl.when(i >= ntile - max(win // tm, 1))
        def _(g=g, take=take):
            k = jnp.concatenate(_rope_heads(_dot(xn, w_ref[:, 2 * g * CW:(2 * g + 1) * CW]), cos_t, sin_t, lt), axis=1)
            v = _dot(xn, w_ref[:, (2 * g + 1) * CW:(2 * g + 2) * CW])
            for j in range(rpt):
                src = k if j < DIL_HEADS else v
                hs = slice((j % DIL_HEADS) * DIL_HD, (j % DIL_HEADS + 1) * DIL_HD)
                o_refs[g][pl.ds(j, take, stride=rpt), :] = src[tm - take:, hs]


def _kvtail_prompt(x, nw, wkv, cos_t, sin_t, tm, wins, outs, layer, depth):
    t, d = x.shape
    n = wkv.shape[1]
    tail = max(wins)
    ntile = tail // tm
    off = (t - tail) // tm
    rpt = 2 * DIL_HEADS
    row = lambda w: pl.BlockSpec((tm, w), lambda i: (i + off, 0))
    ospecs = []
    for win in wins:
        take = min(win, tm)
        first = ntile - max(win // tm, 1)
        ospecs.append(pl.BlockSpec((None, take * rpt, LANES),
                                   lambda i, first=first: (layer, jnp.maximum(i - first, 0), 0)))
    aliased = outs is not None
    anyspec = pl.BlockSpec(memory_space=pl.ANY)
    return pl.pallas_call(
        functools.partial(_kvtail_prompt_kernel, wins=wins, aliased=aliased),
        grid=(ntile,),
        in_specs=[row(d), _resident((1, d)), _resident((d, n)), row(LANES), row(LANES)]
        + ([anyspec] * len(wins) if aliased else []),
        out_specs=ospecs,
        out_shape=[jax.ShapeDtypeStruct((depth, win * rpt, LANES), F32) for win in wins],
        input_output_aliases={5 + g: g for g in range(len(wins))} if aliased else {},
        compiler_params=_params(("arbitrary",)),
        name="kvtail_prompt",
    )(x, nw, wkv, cos_t, sin_t, *(outs if aliased else []))


def _kvtail(x, nw, wkv, cos_t, sin_t, tm):
    m, d = x.shape
    n = wkv.shape[1]
    row = lambda w: pl.BlockSpec((tm, w), lambda i: (i, 0))
    return pl.pallas_call(
        _kvtail_kernel,
        grid=(m // tm,),
        in_specs=[row(d), _resident((1, d)), _resident((d, n)), row(LANES), row(LANES)],
        out_specs=row(n),
        out_shape=jax.ShapeDtypeStruct((m, n), F32),
        compiler_params=_params(("parallel",)),
        name="kvtail",
    )(x, nw, wkv, cos_t, sin_t)


@functools.lru_cache(maxsize=None)
def _gla_consts(C, nl, seg):
    x = np.arange(C)[:, None]
    i = np.arange(C)[None, :]
    same = (x // seg == i // seg) if seg else np.ones((C, C), bool)
    mats = [(i <= x) & same, (i > x) & same]
    masks = [x == i]
    for l in range(nl):
        m = 1 << l
        r = (x | (2 * m - 1)) - m
        mats.append((i > np.minimum(x, r)) & (i <= np.maximum(x, r)))
        masks.append((((x >> l) ^ (i >> l)) == 1) & (((x >> l) & 1) == 1))
    pall = np.concatenate(mats, 0).astype(np.float32)
    return pall, np.stack(masks).astype(np.float32)


def _gla_core(q, k, v, g, pall_ref, mask_ref, nl):
    C = q.shape[0]
    g_hi = g.astype(BF16)
    g_lo = (g - g_hi.astype(F32)).astype(BF16)
    ex = _dot(pall_ref[...], jnp.concatenate([g_hi, g_lo], axis=1))
    ex = ex[:, :GLA_DK] + ex[:, GLA_DK:]
    b = ex[0:C]
    aft = ex[C:2 * C]
    qf = q.astype(F32)
    kf = k.astype(F32)
    a = _dot_nt(q, k) * mask_ref[0]
    for l in range(nl):
        e = jnp.exp(ex[(2 + l) * C:(3 + l) * C])
        a = a + _dot_nt((qf * e).astype(BF16), (kf * e).astype(BF16)) * mask_ref[1 + l]
    o = _dot(a.astype(BF16), v)
    qe = (qf * jnp.exp(b)).astype(BF16)
    kdt = (kf * jnp.exp(aft)).T.astype(BF16)
    return o, qe, kdt, b.T


def _gla_prompt_kernel(*refs, nl, nroll, aliased):
    q_ref, k_ref, v_ref, g_ref, pall_ref, mask_ref = refs[:6]
    roll_in = refs[6:6 + 2 * nroll]
    outs = refs[6 + (3 if aliased else 2) * nroll:]
    o_ref, sf_ref = outs[:2]
    roll_out = outs[2:2 + nroll]
    s_scr = outs[2 + nroll]
    i = pl.program_id(0)

    @pl.when(i == 0)
    def _():
        s_scr[...] = jnp.zeros_like(s_scr)

    for g in range(nroll):
        blk_ref, halo_ref = roll_in[2 * g], roll_in[2 * g + 1]
        rows, drop = blk_ref.shape[0], halo_ref.shape[0]
        roll_out[g][0:rows - drop, :] = blk_ref[drop:rows, :]
        roll_out[g][rows - drop:rows, :] = halo_ref[...]

    C = q_ref.shape[0]
    for h in range(GLA_HEADS):
        ks = slice(h * GLA_DK, (h + 1) * GLA_DK)
        vs = slice(h * GLA_DV, (h + 1) * GLA_DV)
        v = v_ref[:, vs]
        o, qe, kdt, bt = _gla_core(q_ref[:, ks], k_ref[:, ks], v, g_ref[:, ks], pall_ref, mask_ref, nl)
        s = s_scr[h]
        o_ref[:, vs] = (o + _dot(qe, s.astype(BF16))).astype(BF16)
        s_scr[h] = s * jnp.exp(bt[:, C - 1:C]) + _dot(kdt, v)

    @pl.when(i == pl.num_programs(0) - 1)
    def _():
        sf_ref[...] = s_scr[...]


def _gla_prompt(z, la, caches, rolled, layer, drop):
    m = z.shape[0]
    C = GLA_C
    nsteps = m // C
    nl = C.bit_length() - 1
    pall, masks = _gla_consts(C, nl, 0)
    pall = jnp.asarray(pall, BF16)
    masks = jnp.asarray(masks, F32)
    qk = GLA_HEADS * GLA_DK
    va = GLA_HEADS * GLA_DV
    aliased = rolled is not None
    nb = caches[0].shape[1]
    spw = nsteps // nb
    assert spw * nb == nsteps
    roll_in, roll_args, roll_out = [], [], []
    for c in caches:
        wr = c.shape[2]
        rows = wr // spw
        assert rows * spw == wr and rows % drop == 0
        per = rows // drop
        roll_in += [pl.BlockSpec((None, None, rows, LANES), lambda i: (layer, i // spw, i % spw, 0)),
                    pl.BlockSpec((None, None, drop, LANES),
                                 lambda i, per=per, last=wr // drop - 1:
                                 (layer, i // spw, jnp.minimum((i % spw + 1) * per, last), 0))]
        roll_args += [c, c]
        roll_out.append(pl.BlockSpec((None, None, rows, LANES), lambda i: (layer, i // spw, i % spw, 0)))
    anyspec = pl.BlockSpec(memory_space=pl.ANY)
    nroll = len(caches)
    n_in = 6 + 2 * nroll
    outs = pl.pallas_call(
        functools.partial(_gla_prompt_kernel, nl=nl, nroll=nroll, aliased=aliased),
        grid=(nsteps,),
        in_specs=[pl.BlockSpec((C, qk), lambda i: (i, Z_QA)),
                  pl.BlockSpec((C, qk), lambda i: (i, Z_KA)),
                  pl.BlockSpec((C, va), lambda i: (i, Z_VA * CW // va)),
                  pl.BlockSpec((C, qk), lambda i: (i, 0)),
                  _resident(pall.shape), _resident(masks.shape)] + roll_in
        + ([anyspec] * nroll if aliased else []),
        out_specs=[pl.BlockSpec((C, va), lambda i: (i, 0)),
                   pl.BlockSpec((GLA_HEADS, GLA_DK, GLA_DV), lambda i: (0, 0, 0))] + roll_out,
        out_shape=[jax.ShapeDtypeStruct((m, va), BF16),
                   jax.ShapeDtypeStruct((GLA_HEADS, GLA_DK, GLA_DV), F32)]
        + [jax.ShapeDtypeStruct(c.shape, F32) for c in caches],
        scratch_shapes=[pltpu.VMEM((GLA_HEADS, GLA_DK, GLA_DV), F32)],
        input_output_aliases={n_in + g: 2 + g for g in range(nroll)} if aliased else {},
        compiler_params=_params(("arbitrary",)),
        name="gla_prompt",
    )(z, z, z, la, pall, masks, *roll_args, *(rolled if aliased else []))
    return outs[0], outs[1], list(outs[2:])


def _gla_sample_kernel(*refs, nl, tvalid, aliased):
    q_ref, k_ref, v_ref, g_ref, s0_ref, pall_ref, mask_ref = refs[:7]
    o_ref, sf_ref = refs[8:] if aliased else refs[7:]
    C = q_ref.shape[0]
    nseq = C // SROWS
    row = lax.broadcasted_iota(jnp.int32, (C, 1), 0)
    g = jnp.where(row % SROWS < tvalid, g_ref[...], 0.0)
    v = v_ref[...]
    o, qe, kdt, bt = _gla_core(q_ref[...], k_ref[...], v, g, pall_ref, mask_ref, nl)
    zero_v = jnp.zeros_like(v)
    for j in range(nseq):
        mine = row // SROWS == j
        s = s0_ref[j, 0]
        o = o + jnp.where(mine, _dot(qe, s.astype(BF16)), 0.0)
        last = j * SROWS + SROWS - 1
        sf_ref[j, 0] = s * jnp.exp(bt[:, last:last + 1]) + _dot(kdt, jnp.where(mine, v, zero_v))
    o_ref[...] = o.astype(BF16)


def _gla_sample(z, la, states, new_states, layer, tvalid):
    m = z.shape[0]
    C = GLA_C
    nseq = C // SROWS
    nl = (tvalid - 1).bit_length()
    pall, masks = _gla_consts(C, nl, SROWS)
    pall = jnp.asarray(pall, BF16)
    masks = jnp.asarray(masks, F32)
    va = GLA_HEADS * GLA_DV
    sspec = pl.BlockSpec((None, nseq, 1, GLA_DK, GLA_DV), lambda c, h: (layer, c, h, 0, 0))
    aliased = new_states is not None
    return pl.pallas_call(
        functools.partial(_gla_sample_kernel, nl=nl, tvalid=tvalid, aliased=aliased),
        grid=(m // C, GLA_HEADS),
        in_specs=[pl.BlockSpec((C, GLA_DK), lambda c, h: (c, Z_QA * CW // GLA_DK + h)),
                  pl.BlockSpec((C, GLA_DK), lambda c, h: (c, Z_KA * CW // GLA_DK + h)),
                  pl.BlockSpec((C, GLA_DV), lambda c, h: (c, Z_VA * CW // GLA_DV + h)),
                  pl.BlockSpec((C, GLA_DK), lambda c, h: (c, h)),
                  sspec, _resident(pall.shape), _resident(masks.shape)]
        + ([pl.BlockSpec(memory_space=pl.ANY)] if aliased else []),
        out_specs=[pl.BlockSpec((C, GLA_DV), lambda c, h: (c, h)), sspec],
        out_shape=[jax.ShapeDtypeStruct((m, va), BF16),
                   jax.ShapeDtypeStruct(states.shape, F32)],
        input_output_aliases={7: 1} if aliased else {},
        compiler_params=_params(("parallel", "parallel")),
        name="gla_sample",
    )(z, z, z, la, states, pall, masks, *([new_states] if aliased else []))


def _dil_prompt_kernel(q_ref, k_ref, kp_ref, v_ref, vp_ref, o_ref, lse_ref):
    i = pl.program_id(1)
    rb = q_ref.shape[0]
    nk = 2 * DIL_STEPS
    qi = lax.broadcasted_iota(jnp.int32, (DIL_STEPS, nk), 0)
    ki = lax.broadcasted_iota(jnp.int32, (DIL_STEPS, nk), 1)
    lane = lax.broadcasted_iota(jnp.int32, (DIL_STEPS, LANES), 1)
    band = (ki >= qi) & (ki <= qi + DIL_STEPS)
    bias = jnp.where(band, 0.0, NEG)
    bias_first = jnp.where((i > 0) | (ki >= DIL_STEPS), bias, NEG)
    ones = jnp.ones((nk, DIL_HD), BF16)
    for sb in range(rb // DIL_STEPS):
        rows = slice(sb * DIL_STEPS, (sb + 1) * DIL_STEPS)
        lse_tile = jnp.zeros((DIL_STEPS, LANES), F32)
        for h in range(DIL_HEADS):
            hs = slice(h * DIL_HD, (h + 1) * DIL_HD)
            if sb == 0:
                kcat = jnp.concatenate([kp_ref[:, hs], k_ref[rows, hs]], axis=0)
                vcat = jnp.concatenate([vp_ref[:, hs], v_ref[rows, hs]], axis=0)
            else:
                krows = slice((sb - 1) * DIL_STEPS, (sb + 1) * DIL_STEPS)
                kcat, vcat = k_ref[krows, hs], v_ref[krows, hs]
            s = _dot_nt(q_ref[rows, hs], kcat) + (bias_first if sb == 0 else bias)
            mx = jnp.max(s, axis=-1, keepdims=True)
            p = jnp.exp(s - mx).astype(BF16)
            pv = _dot(p, jnp.concatenate([vcat, ones], axis=1))
            den = pv[:, DIL_HD:]
            o_ref[rows, hs] = (pv[:, :DIL_HD] / den).astype(BF16)
            lse_tile = jnp.where(lane == h, mx + jnp.log(den), lse_tile)
        lse_ref[rows, :] = lse_tile


def _dil_prompt(qkv, rb):
    dil, l, _ = qkv.shape
    nprev = rb // DIL_STEPS
    own = lambda c: pl.BlockSpec((None, rb, CW), lambda r, i: (r, i, c))
    prev = lambda c: pl.BlockSpec((None, DIL_STEPS, CW), lambda r, i: (r, jnp.maximum(i * nprev - 1, 0), c))
    return pl.pallas_call(
        _dil_prompt_kernel,
        grid=(dil, l // rb),
        in_specs=[own(0), own(1), prev(1), own(2), prev(2)],
        out_specs=[pl.BlockSpec((None, rb, CW), lambda r, i: (r, i, 0)),
                   pl.BlockSpec((None, rb, LANES), lambda r, i: (r, i, 0))],
        out_shape=[jax.ShapeDtypeStruct((dil, l, CW), BF16),
                   jax.ShapeDtypeStruct((dil, l, LANES), F32)],
        compiler_params=_params(("parallel", "parallel")),
        name=f"dil_prompt_d{dil}",
    )(qkv, qkv, qkv, qkv, qkv)


NUNIT = DEC_T * DIL_HEADS
KEYS = DIL_STEPS * 2 * DIL_HEADS + LANES


def _dil_sample_kernel(q_ref, kvn_ref, c0_ref, c1_ref, c2_ref, a0_ref, a1_ref, a2_ref,
                       o_ref, n0_ref, n1_ref, n2_ref, *, bb, tvalid):
    del a0_ref, a1_ref, a2_ref
    rpt = 2 * DIL_HEADS
    nrow = DIL_STEPS * rpt
    qall = q_ref[...]
    kvn = kvn_ref[...]
    row_u = lax.broadcasted_iota(jnp.int32, (NUNIT, KEYS), 0)
    key = lax.broadcasted_iota(jnp.int32, (NUNIT, KEYS), 1)
    t_u = row_u // DIL_HEADS
    mine = key % rpt == row_u % DIL_HEADS
    tok = jnp.where(key < nrow, key // rpt, (key - nrow) // rpt)
    bias_d1 = jnp.where(mine, jnp.where(key < nrow, jnp.where(tok >= t_u, 0.0, NEG),
                                        jnp.where(tok <= t_u, 0.0, NEG)), NEG)
    bias_dn = jnp.where(mine, jnp.where(key < nrow, 0.0, jnp.where(tok == t_u, 0.0, NEG)), NEG)
    t_row = lax.broadcasted_iota(jnp.int32, (NUNIT, DIL_HD), 0) // DIL_HEADS
    t_col = lax.broadcasted_iota(jnp.int32, (NUNIT, 1), 0) // DIL_HEADS
    c_refs = (c0_ref, c1_ref, c2_ref)
    n_refs = (n0_ref, n1_ref, n2_ref)

    passes = []
    scores = []
    for b in range(bb):
        for g, (_, dil) in enumerate(DIL_PAIRS):
            c_ref = c_refs[g]
            new_rows = []
            for t in range(tvalid):
                r = b * SROWS + t
                blk = jnp.concatenate(
                    [kvn[r:r + 1, g * 2 * CW + j * DIL_HD:g * 2 * CW + (j + 1) * DIL_HD] for j in range(rpt)],
                    axis=0)
                n_refs[g][b, t * rpt:(t + 1) * rpt, :] = blk
                new_rows.append(blk)
            new_rows.append(jnp.zeros((KEYS - nrow - tvalid * rpt, DIL_HD), F32))
            q_rows = jnp.concatenate(
                [qall[b * SROWS + t:b * SROWS + t + 1, g * CW + h * DIL_HD:g * CW + (h + 1) * DIL_HD]
                 for t in range(DEC_T) for h in range(DIL_HEADS)], axis=0)
            for t in (range(DEC_T) if dil > 1 else (None,)):
                rows = c_ref[b, :, 0 if t is None else t].reshape(nrow, DIL_HD)
                kv = jnp.concatenate([rows] + new_rows, axis=0).astype(BF16)
                q = q_rows if t is None else jnp.where(t_row == t, q_rows, 0.0)
                scores.append(_dot_nt(q.astype(BF16), kv) + (bias_d1 if t is None else bias_dn))
                passes.append((b, g, t, kv))
    s = jnp.concatenate(scores, axis=0)
    mx = jnp.max(s, axis=-1, keepdims=True)
    p = jnp.exp(s - mx)
    den = jnp.sum(p, axis=-1, keepdims=True)
    lse = mx + jnp.log(den)
    pv = pltpu.roll(p, DIL_HEADS, axis=1).astype(BF16)

    outs = [[jnp.zeros((NUNIT, DIL_HD), F32)] * N_GROUPS for _ in range(bb)]
    lses = [[jnp.zeros((NUNIT, 1), F32)] * N_GROUPS for _ in range(bb)]
    for i, (b, g, t, kv) in enumerate(passes):
        rs = slice(i * NUNIT, (i + 1) * NUNIT)
        o = _dot(pv[rs], kv) / den[rs]
        if t is None:
            outs[b][g], lses[b][g] = o, lse[rs]
        else:
            outs[b][g] = jnp.where(t_row == t, o, outs[b][g])
            lses[b][g] = jnp.where(t_col == t, lse[rs], lses[b][g])
    for b in range(bb):
        lb, ob = lses[b], outs[b]
        mxl = jnp.maximum(jnp.maximum(lb[0], lb[1]), lb[2])
        es = [jnp.exp(l - mxl) for l in lb]
        inv = 1.0 / (es[0] + es[1] + es[2])
        merged = (es[0] * inv) * ob[0] + (es[1] * inv) * ob[1] + (es[2] * inv) * ob[2]
        orow = [jnp.concatenate([merged[t * DIL_HEADS + h:t * DIL_HEADS + h + 1, :] for h in range(DIL_HEADS)],
                                axis=1) for t in range(DEC_T)]
        o_ref[b * SROWS:(b + 1) * SROWS, :] = jnp.concatenate(
            orow + [jnp.zeros((SROWS - DEC_T, DIL_OUT), F32)], axis=0)


def _dil_sample(qf, kvn, caches, rolled, layer, tvalid, bb):
    assert tvalid == DEC_T
    m = qf.shape[0]
    nb = caches[0].shape[1]
    rows = bb * SROWS
    rpt = 2 * DIL_HEADS
    views, cspecs, nspecs = [], [], []
    for c, (_, dil) in zip(caches, DIL_PAIRS):
        depth, _, wr, _ = c.shape
        views.append(c.reshape(depth, nb, DIL_STEPS, dil, rpt, LANES))
        cspecs.append(pl.BlockSpec((None, bb, DIL_STEPS, min(dil, DEC_T), rpt, LANES),
                                   lambda i: (layer, i, 0, 0, 0, 0)))
        nspecs.append(pl.BlockSpec((None, bb, tvalid * rpt, LANES),
                                   lambda i, last=wr // (tvalid * rpt) - 1: (layer, i, last, 0)))
    anyspec = pl.BlockSpec(memory_space=pl.ANY)
    outs = pl.pallas_call(
        functools.partial(_dil_sample_kernel, bb=bb, tvalid=tvalid),
        grid=(nb // bb,),
        in_specs=[pl.BlockSpec((rows, N_GROUPS * CW), lambda i: (i, 0)),
                  pl.BlockSpec((rows, N_GROUPS * 2 * CW), lambda i: (i, 0))] + cspecs + [anyspec] * N_GROUPS,
        out_specs=[pl.BlockSpec((rows, CW), lambda i: (i, 0))] + nspecs,
        out_shape=[jax.ShapeDtypeStruct((m, CW), F32)] + [jax.ShapeDtypeStruct(r.shape, F32) for r in rolled],
        input_output_aliases={2 + N_GROUPS + g: 1 + g for g in range(N_GROUPS)},
        compiler_params=_params(("parallel",)),
        name="dil_sample",
    )(qf, kvn, *views, *rolled)
    return outs[0], list(outs[1:])


def _tokens_from_streams(ref, scr):
    dil, n, w = ref.shape
    if dil == 1:
        return ref[0].astype(F32)
    nslab = w // LANES
    for r in range(dil):
        blk = ref[r].astype(F32)
        for j in range(nslab):
            scr[j, pl.ds(r, n, stride=dil), :] = blk[:, j * LANES:(j + 1) * LANES]
    return jnp.concatenate([scr[j] for j in range(nslab)], axis=1)


def _merge_kernel(*refs, grouped):
    if grouped:
        (x_ref, oa_ref, r_ref, ga_ref, gb_ref, o0_ref, o1_ref, o2_ref, l0_ref, l1_ref, l2_ref,
         gn_ref, wa_ref, wb_ref, wo_ref, nw_ref, h_ref, o_scr, l_scr) = refs
    else:
        x_ref, oa_ref, r_ref, ga_ref, gb_ref, ob_ref, gn_ref, wa_ref, wb_ref, wo_ref, nw_ref, h_ref = refs
    oa = oa_ref[...].astype(F32)
    gn = gn_ref[...]
    parts = [_rms(oa[:, h * GLA_DV:(h + 1) * GLA_DV], gn) for h in range(GLA_HEADS)]
    r = r_ref[...].astype(F32)
    oan = (jnp.concatenate(parts, axis=1) * (r * jax.nn.sigmoid(r))).astype(BF16)

    if grouped:
        lses = [_tokens_from_streams(l, l_scr) for l in (l0_ref, l1_ref, l2_ref)]
        mx = jnp.maximum(jnp.maximum(lses[0], lses[1]), lses[2])
        es = [jnp.exp(l - mx) for l in lses]
        inv = 1.0 / (es[0] + es[1] + es[2])
        outs = [_tokens_from_streams(o, o_scr) for o in (o0_ref, o1_ref, o2_ref)]
        ob = []
        for h in range(DIL_HEADS):
            hs = slice(h * DIL_HD, (h + 1) * DIL_HD)
            acc = None
            for e, o in zip(es, outs):
                term = (e * inv)[:, h:h + 1] * o[:, hs]
                acc = term if acc is None else acc + term
            ob.append(acc)
        ob = jnp.concatenate(ob, axis=1).astype(BF16)
    else:
        ob = ob_ref[...].astype(BF16)

    merged = (ga_ref[...].astype(F32) * _dot(oan, wa_ref[...])
              + gb_ref[...].astype(F32) * _dot(ob, wb_ref[...]))
    y = _dot(merged.astype(BF16), wo_ref[...])
    h_ref[...] = x_ref[...] + _rms(y, nw_ref[...])


def _merge(x, oa, z, dil_o, dil_lse, gn, wa, wb, wo, nw, tm):
    m, d = x.shape
    va = oa.shape[1]
    grouped = dil_lse is not None
    zc = lambda c: pl.BlockSpec((tm, d), lambda i: (i, c * CW // d))
    row = lambda w: pl.BlockSpec((tm, w), lambda i: (i, 0))
    stream = lambda a: pl.BlockSpec((a.shape[0], tm // a.shape[0], a.shape[2]), lambda i: (0, i, 0))
    if grouped:
        dil_specs = [stream(a) for a in dil_o] + [stream(a) for a in dil_lse]
        dil_args = [*dil_o, *dil_lse]
        scratch = [pltpu.VMEM((CW // LANES, tm, LANES), F32), pltpu.VMEM((1, tm, LANES), F32)]
    else:
        dil_specs, dil_args, scratch = [row(CW)], [dil_o], []
    return pl.pallas_call(
        functools.partial(_merge_kernel, grouped=grouped),
        grid=(m // tm,),
        in_specs=[row(d), row(va), zc(Z_RA), zc(Z_GA), zc(Z_GB)] + dil_specs
        + [_resident(gn.shape), _resident(wa.shape), _resident(wb.shape),
           _resident(wo.shape), _resident(nw.shape)],
        out_specs=row(d),
        out_shape=jax.ShapeDtypeStruct((m, d), F32),
        scratch_shapes=scratch,
        compiler_params=_params(("parallel",)),
        name="merge",
    )(x, oa, z, z, z, *dil_args, gn, wa, wb, wo, nw)


def _gelu_tanh(x):
    return 0.5 * x * (1.0 + jnp.tanh(0.7978845608028654 * (x + 0.044715 * x * x * x)))


def _ffn_kernel(*refs, fc, sample):
    if sample:
        h_ref, nw1_ref, wup_ref, wc_ref, bc_ref, wdn_ref, nw2_ref, p1_ref, p2_ref, out_ref, gst_ref = refs
    else:
        h_ref, nw1_ref, wup_ref, wc_ref, bc_ref, wdn_ref, nw2_ref, out_ref, gst_ref, cbuf = refs
    tm = h_ref.shape[0]
    dff = wdn_ref.shape[0]
    i = pl.program_id(0)

    if not sample:
        @pl.when(i == 0)
        def _():
            cbuf[...] = jnp.zeros_like(cbuf)

    h = h_ref[...]
    hn = _rms(h, nw1_ref[...]).astype(BF16)
    if sample:
        tpos = lax.broadcasted_iota(jnp.int32, (tm, 1), 0) % SROWS
    else:
        row8 = lax.broadcasted_iota(jnp.int32, (SUBLANES, 1), 0)
    acc = None
    for c in range(dff // fc):
        cs = slice(c * fc, (c + 1) * fc)
        gp = _dot(hn, wup_ref[:, cs])
        val = _dot(hn, wup_ref[:, dff + c * fc:dff + (c + 1) * fc])
        g1 = pltpu.roll(gp, 1, axis=0)
        g2 = pltpu.roll(gp, 2, axis=0)
        if sample:
            g1 = jnp.where(tpos == 0, p1_ref[:, cs], g1)
            g2 = jnp.where(tpos < 2, p2_ref[:, cs], g2)
            gst_ref[:, cs] = gp
        else:
            carry = cbuf[:, cs]
            head1 = jnp.where(row8 < 1, pltpu.roll(carry, 1, axis=0), g1[0:SUBLANES])
            head2 = jnp.where(row8 < 2, pltpu.roll(carry, 2, axis=0), g2[0:SUBLANES])
            g1 = jnp.concatenate([head1, g1[SUBLANES:]], axis=0)
            g2 = jnp.concatenate([head2, g2[SUBLANES:]], axis=0)
            cbuf[:, cs] = gp[tm - SUBLANES:]
        conv = g2 * wc_ref[0:1, cs] + g1 * wc_ref[1:2, cs] + gp * wc_ref[2:3, cs] + bc_ref[:, cs]
        act = (_gelu_tanh(conv) * val).astype(BF16)
        part = _dot(act, wdn_ref[cs, :])
        acc = part if acc is None else acc + part
    if not sample:
        gst_ref[...] = cbuf[...]
    out_ref[...] = h + _rms(acc, nw2_ref[...])


def _ffn(h, nw1, wup, wc, bc, wdn, nw2, prev, tm, fc):
    m, d = h.shape
    dff = wdn.shape[0]
    sample = prev is not None
    row = lambda w: pl.BlockSpec((tm, w), lambda i: (i, 0))
    in_specs = [row(d), _resident((1, d)), _resident(wup.shape), _resident(wc.shape),
                _resident((1, dff)), _resident(wdn.shape), _resident((1, d))]
    args = [h, nw1, wup, wc, bc, wdn, nw2]
    out_specs, out_shape, scratch = [row(d)], [jax.ShapeDtypeStruct((m, d), F32)], []
    if sample:
        in_specs += [row(dff), row(dff)]
        args += list(prev)
        out_specs.append(row(dff))
        out_shape.append(jax.ShapeDtypeStruct((m, dff), F32))
    else:
        out_specs.append(pl.BlockSpec((SUBLANES, dff), lambda i: (i, 0)))
        out_shape.append(jax.ShapeDtypeStruct((m // tm * SUBLANES, dff), F32))
        scratch.append(pltpu.VMEM((SUBLANES, dff), F32))
    return pl.pallas_call(
        functools.partial(_ffn_kernel, fc=fc, sample=sample),
        grid=(m // tm,),
        in_specs=in_specs,
        out_specs=out_specs,
        out_shape=out_shape,
        scratch_shapes=scratch,
        compiler_params=_params(("arbitrary",)),
        name="ffn_sample" if sample else "ffn_prompt",
    )(*args)


def _rope_tables(pos):
    inv = ROPE_THETA ** (-jnp.arange(ROPE_HALF, dtype=F32) * 2.0 / ROPE_DIM)
    ang = pos[:, None] * inv[None, :]
    cos, sin = jnp.cos(ang), jnp.sin(ang)
    pad = LANES - ROPE_DIM
    one = jnp.ones((pos.shape[0], pad), F32)
    cos_t = jnp.concatenate([cos, cos, one], axis=1)
    sin_t = jnp.concatenate([-sin, sin, 0.0 * one], axis=1)
    return cos_t, sin_t


def _prep_in(w_in_l, d):
    qk = GLA_HEADS * GLA_DK
    va = GLA_HEADS * GLA_DV
    nb = N_GROUPS * DIL_OUT
    sizes = (qk, qk, va, va, GLA_RANK, 3 * nb, 2 * d)
    offs = np.cumsum((0,) + sizes)
    q_a, k_a, v_a, r_a, a_low, qkv_b, gate = (w_in_l[:, offs[j]:offs[j + 1]] for j in range(7))
    q_b, k_b, v_b = qkv_b[:, :nb], qkv_b[:, nb:2 * nb], qkv_b[:, 2 * nb:]
    wz = jnp.concatenate([q_a, k_a, v_a, r_a, gate, q_b, k_b, v_b], axis=1).astype(BF16)
    wal = jnp.pad(a_low, ((0, 0), (0, LANES - GLA_RANK))).astype(BF16)
    kv = []
    for g in range(N_GROUPS):
        kv += [k_b[:, g * DIL_OUT:(g + 1) * DIL_OUT], v_b[:, g * DIL_OUT:(g + 1) * DIL_OUT]]
    wkv = jnp.concatenate(kv, axis=1).astype(BF16)
    return wz, wal, wkv


def kernel(x_prompt, x_sample, state_gla, cache_dil_w128, cache_dil_w512, cache_dil_w2048, state_ffn_conv,
           w_in, w_gate_up, b_gate, gla_norm, w_branch_a, w_branch_b, w_out, norm_pre_mix, norm_post_mix,
           norm_pre_ffn, norm_post_ffn, w_ffn_up, w_conv, b_conv, w_ffn_down):
    depth, d = w_in.shape[0], w_in.shape[1]
    _, t_p, _ = x_prompt.shape
    nb, t_s, _ = x_sample.shape
    dff = w_ffn_down.shape[1]
    caches = (cache_dil_w128, cache_dil_w512, cache_dil_w2048)
    assert x_prompt.shape[0] == 1 and t_s <= SROWS // 2 and d == 2 * DIL_OUT
    for c, (win, _) in zip(caches, DIL_PAIRS):
        assert c.shape[2] == win and t_p >= win
    tail = max(w for w, _ in DIL_PAIRS)
    dils = tuple(dl for _, dl in DIL_PAIRS)
    tm = 256
    tm_big = min(512, t_p)
    m_s = nb * SROWS
    tm_s = min(tm, m_s)
    tm_ffn_s = min(128, m_s)

    xp = x_prompt.reshape(t_p, d)
    xs = jnp.pad(x_sample, ((0, 0), (0, SROWS - t_s), (0, 0))).reshape(nb * SROWS, d)
    cos_p, sin_p = _rope_tables(jnp.arange(t_p, dtype=F32))
    pos_s = PAST_LEN + jnp.minimum(jnp.arange(SROWS), t_s - 1).astype(F32)
    cos_s, sin_s = (jnp.tile(a, (nb, 1)) for a in _rope_tables(pos_s))
    cache_v = [c.reshape(depth, nb, c.shape[2] * 2 * DIL_HEADS, DIL_HD) for c in caches]
    new_cache = None

    p_gla, p_conv, s_conv = [], [], []
    p_kv = s_gla = None
    wins = tuple(w for w, _ in DIL_PAIRS)
    row2 = lambda a: a.reshape(1, -1)
    for i in range(depth):
        wz, wal, wkv = _prep_in(w_in[i], d)
        wgu = jnp.pad(w_gate_up[i], ((0, LANES - GLA_RANK), (0, 0))).astype(BF16)
        bg = row2(b_gate[i])
        wa, wb, wo = (w.astype(BF16) for w in (w_branch_a[i], w_branch_b[i], w_out[i]))
        wup, wdn = w_ffn_up[i].astype(BF16), w_ffn_down[i].astype(BF16)
        n1, n2, n3, n4 = (row2(n[i]) for n in (norm_pre_mix, norm_post_mix, norm_pre_ffn, norm_post_ffn))
        gn = row2(gla_norm[i])
        wc = jnp.pad(w_conv[i], ((0, SUBLANES - CONV_W), (0, 0)))
        bc = row2(b_conv[i])

        z, la, *qkv = _inproj(xp, n1, wz, wal, wgu, bg, cos_p, sin_p, tm_big, dils)
        p_kv = _kvtail_prompt(xp, n1, wkv, cos_p, sin_p, tm, wins, p_kv, i, depth)
        oa, sfin, new_cache = _gla_prompt(z, la, cache_v, new_cache, i, t_s * 2 * DIL_HEADS)
        dil = [_dil_prompt(a, min(1024, a.shape[1])) for a in qkv]
        h = _merge(xp, oa, z, [o for o, _ in dil], [l for _, l in dil], gn, wa, wb, wo, n2, tm_big)
        xp, gst = _ffn(h, n3, wup, wc, bc, wdn, n4, None, tm_big, 1024)
        p_gla.append(sfin[None])
        p_conv.append(gst[-(CONV_W - 1):][None])

        z, la, *qkv = _inproj(xs, n1, wz, wal, wgu, bg, cos_s, sin_s, tm_s, (1,) * N_GROUPS)
        kvn = _kvtail(xs, n1, wkv, cos_s, sin_s, tm_s)
        oa, s_gla = _gla_sample(z, la, state_gla, s_gla, i, t_s)
        qf = jnp.concatenate([a[0, :, :CW] for a in qkv], axis=1).astype(F32)
        ob, new_cache = _dil_sample(qf, kvn, cache_v, new_cache, i, t_s, min(2, nb))
        h = _merge(xs, oa, z, ob, None, gn, wa, wb, wo, n2, tm_s)
        st = state_ffn_conv[i]
        zpad = lambda a: jnp.pad(a, ((0, 0), (0, SROWS - a.shape[1]), (0, 0))).reshape(m_s, dff)
        prev = (zpad(st[:, 1:2]), zpad(st))
        xs, gpre = _ffn(h, n3, wup, wc, bc, wdn, n4, prev, tm_ffn_s, 512)
        s_conv.append(gpre.reshape(nb, SROWS, dff)[:, t_s - (CONV_W - 1):t_s])

    y_p = xp.reshape(1, t_p, d)
    y_s = xs.reshape(nb, SROWS, d)[:, :t_s]
    p_caches = [kv.reshape(depth, 1, win, 2, DIL_HEADS, DIL_HD) for kv, win in zip(p_kv, wins)]
    s_caches = [c.reshape(o.shape) for c, o in zip(new_cache, caches)]
    return (y_p, y_s, jnp.stack(p_gla), *p_caches, jnp.stack(p_conv),
            s_gla, *s_caches, jnp.stack(s_conv))
```

```python
import functools

import numpy as np
import jax
import jax.numpy as jnp
from jax import lax
from jax.experimental import pallas as pl
from jax.experimental.pallas import tpu as pltpu

F32 = jnp.float32
BF16 = jnp.bfloat16

GLA_HEADS = 4
GLA_DK = 128
GLA_DV = 256
GLA_RANK = 16
GLA_TAU = 16.0
DIL_PAIRS = ((128, 1), (512, 4), (2048, 16))
N_GROUPS = len(DIL_PAIRS)
DIL_HEADS = 4
DIL_HD = 128
DIL_STEPS = 128
DIL_OUT = DIL_HEADS * DIL_HD
ROPE_THETA = 500000.0
ROPE_DIM = DIL_HD // 4
ROPE_HALF = ROPE_DIM // 2
CONV_W = 3
EPS = 1e-6
PAST_LEN = 16384

LANES = 128
SUBLANES = 8
VMEM_LIMIT = 56 * 1024 * 1024

CW = 512
Z_QA, Z_KA, Z_VA, Z_RA, Z_GA, Z_GB, Z_QB = 0, 1, 2, 4, 6, 8, 10
Z_BLOCKS = 19
SROWS = 8
DEC_T = 4
GLA_C = 128
GLA_CHUNKS_PER_STEP = 2
DIL_ROWS_PER_STEP = 2048
NEG = -1e30


def _params(sem):
    return pltpu.CompilerParams(dimension_semantics=sem, vmem_limit_bytes=VMEM_LIMIT)


def _resident(shape):
    nd = len(shape)
    return pl.BlockSpec(shape, lambda *_: (0,) * nd, pipeline_mode=pl.Buffered(1))


def _rms(x, w):
    ms = jnp.mean(x * x, axis=-1, keepdims=True)
    return x * lax.rsqrt(ms + EPS) * w


def _dot(a, b):
    return jnp.dot(a, b, preferred_element_type=F32)


def _dot_nt(a, b):
    return lax.dot_general(a, b, (((1,), (1,)), ((), ())), preferred_element_type=F32)


def _rope_heads(acc, cos_t, sin_t, lt):
    parts = []
    for h in range(DIL_HEADS):
        a = acc[:, h * LANES:(h + 1) * LANES]
        up = pltpu.roll(a, LANES - ROPE_HALF, axis=1)
        dn = pltpu.roll(a, ROPE_HALF, axis=1)
        parts.append(a * cos_t + jnp.where(lt, up, dn) * sin_t)
    return parts


def _inproj_kernel(x_ref, nw_ref, wz_ref, wal_ref, wgu_ref, bg_ref, cos_ref, sin_ref,
                   z_ref, la_ref, g0_ref, g1_ref, g2_ref, scr, *, dils):
    tm = x_ref.shape[0]
    xn = _rms(x_ref[...], nw_ref[...]).astype(BF16)
    cos_t = cos_ref[...]
    sin_t = sin_ref[...]
    lt = lax.broadcasted_iota(jnp.int32, cos_t.shape, 1) < ROPE_HALF
    grefs = (g0_ref, g1_ref, g2_ref)
    for c in range(Z_BLOCKS):
        acc = _dot(xn, wz_ref[:, c * CW:(c + 1) * CW])
        if c < Z_QB:
            if c == Z_QA:
                acc = acc * (GLA_DK ** -0.5)
            elif c >= Z_GA:
                acc = jax.nn.sigmoid(acc)
            z_ref[:, c * CW:(c + 1) * CW] = acc.astype(BF16)
            continue
        part, g = divmod(c - Z_QB, N_GROUPS)
        if part < 2:
            acc = jnp.concatenate(_rope_heads(acc, cos_t, sin_t, lt), axis=1)
        if part == 0:
            acc = acc * (DIL_HD ** -0.5)
        cols = slice(part * CW, (part + 1) * CW)
        dil = dils[g]
        if dil == 1:
            grefs[g][0, :, cols] = acc.astype(BF16)
        else:
            for j in range(CW // LANES):
                scr[j] = acc[:, j * LANES:(j + 1) * LANES]
            for r in range(dil):
                rows = [scr[j, pl.ds(r, tm // dil, stride=dil), :] for j in range(CW // LANES)]
                grefs[g][r, :, cols] = jnp.concatenate(rows, axis=1).astype(BF16)
    al = _dot(xn, wal_ref[...]).astype(BF16)
    lg = _dot(al, wgu_ref[...]) + bg_ref[...]
    la_ref[...] = (jnp.minimum(lg, 0.0) - jnp.log1p(jnp.exp(-jnp.abs(lg)))) * (1.0 / GLA_TAU)


def _inproj(x, nw, wz, wal, wgu, bg, cos_t, sin_t, tm, dils):
    m, d = x.shape
    nw_all = wz.shape[1]
    nz = Z_QB * CW
    qk = wgu.shape[1]
    row = lambda w: pl.BlockSpec((tm, w), lambda i: (i, 0))
    gspec = lambda dl: pl.BlockSpec((dl, tm // dl, 3 * CW), lambda i: (0, i, 0))
    return pl.pallas_call(
        functools.partial(_inproj_kernel, dils=dils),
        grid=(m // tm,),
        in_specs=[row(d), _resident((1, d)), _resident((d, nw_all)), _resident((d, LANES)),
                  _resident((LANES, qk)), _resident((1, qk)), row(LANES), row(LANES)],
        out_specs=[row(nz), row(qk)] + [gspec(dl) for dl in dils],
        out_shape=[jax.ShapeDtypeStruct((m, nz), BF16), jax.ShapeDtypeStruct((m, qk), F32)]
        + [jax.ShapeDtypeStruct((dl, m // dl, 3 * CW), BF16) for dl in dils],
        scratch_shapes=[pltpu.VMEM((CW // LANES, tm, LANES), F32)],
        compiler_params=_params(("parallel",)),
        name="inproj",
    )(x, nw, wz, wal, wgu, bg, cos_t, sin_t)


def _kvtail_kernel(x_ref, nw_ref, w_ref, cos_ref, sin_ref, o_ref):
    xn = _rms(x_ref[...], nw_ref[...]).astype(BF16)
    cos_t = cos_ref[...]
    sin_t = sin_ref[...]
    lt = lax.broadcasted_iota(jnp.int32, cos_t.shape, 1) < ROPE_HALF
    for c in range(2 * N_GROUPS):
        acc = _dot(xn, w_ref[:, c * CW:(c + 1) * CW])
        if c % 2 == 0:
            for h, p in enumerate(_rope_heads(acc, cos_t, sin_t, lt)):
                o_ref[:, c * CW + h * LANES:c * CW + (h + 1) * LANES] = p
        else:
            o_ref[:, c * CW:(c + 1) * CW] = acc


def _kvtail_prompt_kernel(*refs, wins, aliased):
    x_ref, nw_ref, w_ref, cos_ref, sin_ref = refs[:5]
    o_refs = refs[5 + (len(wins) if aliased else 0):]
    tm = x_ref.shape[0]
    i = pl.program_id(0)
    ntile = pl.num_programs(0)
    xn = _rms(x_ref[...], nw_ref[...]).astype(BF16)
    cos_t = cos_ref[...]
    sin_t = sin_ref[...]
    lt = lax.broadcasted_iota(jnp.int32, cos_t.shape, 1) < ROPE_HALF
    rpt = 2 * DIL_HEADS
    for g, win in enumerate(wins):
        take = min(win, tm)

        @pl.when(i >= ntile - max(win // tm, 1))
        def _(g=g, take=take):
            k = jnp.concatenate(_rope_heads(_dot(xn, w_ref[:, 2 * g * CW:(2 * g + 1) * CW]), cos_t, sin_t, lt), axis=1)
            v = _dot(xn, w_ref[:, (2 * g + 1) * CW:(2 * g + 2) * CW])
            for j in range(rpt):
                src = k if j < DIL_HEADS else v
                hs = slice((j % DIL_HEADS) * DIL_HD, (j % DIL_HEADS + 1) * DIL_HD)
                o_refs[g][pl.ds(j, take, stride=rpt), :] = src[tm - take:, hs]


def _kvtail_prompt(x, nw, wkv, cos_t, sin_t, tm, wins, outs, layer, depth):
    t, d = x.shape
    n = wkv.shape[1]
    tail = max(wins)
    ntile = tail // tm
    off = (t - tail) // tm
    rpt = 2 * DIL_HEADS
    row = lambda w: pl.BlockSpec((tm, w), lambda i: (i + off, 0))
    ospecs = []
    for win in wins:
        take = min(win, tm)
        first = ntile - max(win // tm, 1)
        ospecs.append(pl.BlockSpec((None, take * rpt, LANES),
                                   lambda i, first=first: (layer, jnp.maximum(i - first, 0), 0)))
    aliased = outs is not None
    anyspec = pl.BlockSpec(memory_space=pl.ANY)
    return pl.pallas_call(
        functools.partial(_kvtail_prompt_kernel, wins=wins, aliased=aliased),
        grid=(ntile,),
        in_specs=[row(d), _resident((1, d)), _resident((d, n)), row(LANES), row(LANES)]
        + ([anyspec] * len(wins) if aliased else []),
        out_specs=ospecs,
        out_shape=[jax.ShapeDtypeStruct((depth, win * rpt, LANES), F32) for win in wins],
        input_output_aliases={5 + g: g for g in range(len(wins))} if aliased else {},
        compiler_params=_params(("arbitrary",)),
        name="kvtail_prompt",
    )(x, nw, wkv, cos_t, sin_t, *(outs if aliased else []))


def _kvtail(x, nw, wkv, cos_t, sin_t, tm):
    m, d = x.shape
    n = wkv.shape[1]
    row = lambda w: pl.BlockSpec((tm, w), lambda i: (i, 0))
    return pl.pallas_call(
        _kvtail_kernel,
        grid=(m // tm,),
        in_specs=[row(d), _resident((1, d)), _resident((d, n)), row(LANES), row(LANES)],
        out_specs=row(n),
        out_shape=jax.ShapeDtypeStruct((m, n), F32),
        compiler_params=_params(("parallel",)),
        name="kvtail",
    )(x, nw, wkv, cos_t, sin_t)


@functools.lru_cache(maxsize=None)
def _gla_consts(C, nl, seg):
    x = np.arange(C)[:, None]
    i = np.arange(C)[None, :]
    same = (x // seg == i // seg) if seg else np.ones((C, C), bool)
    mats = [(i <= x) & same, (i > x) & same]
    masks = [x == i]
    for l in range(nl):
        m = 1 << l
        r = (x | (2 * m - 1)) - m
        mats.append((i > np.minimum(x, r)) & (i <= np.maximum(x, r)))
        masks.append((((x >> l) ^ (i >> l)) == 1) & (((x >> l) & 1) == 1))
    pall = np.concatenate(mats, 0).astype(np.float32)
    return pall, np.stack(masks).astype(np.float32)


def _gla_core(q, k, v, g, pall_ref, mask_ref, nl):
    C = q.shape[0]
    g_hi = g.astype(BF16)
    g_lo = (g - g_hi.astype(F32)).astype(BF16)
    ex = _dot(pall_ref[...], jnp.concatenate([g_hi, g_lo], axis=1))
    ex = ex[:, :GLA_DK] + ex[:, GLA_DK:]
    b = ex[0:C]
    aft = ex[C:2 * C]
    qf = q.astype(F32)
    kf = k.astype(F32)
    a = _dot_nt(q, k) * mask_ref[0]
    for l in range(nl):
        e = jnp.exp(ex[(2 + l) * C:(3 + l) * C])
        a = a + _dot_nt((qf * e).astype(BF16), (kf * e).astype(BF16)) * mask_ref[1 + l]
    o = _dot(a.astype(BF16), v)
    qe = (qf * jnp.exp(b)).astype(BF16)
    kdt = (kf * jnp.exp(aft)).T.astype(BF16)
    return o, qe, kdt, b.T


def _gla_prompt_kernel(*refs, nl, nroll, aliased):
    q_ref, k_ref, v_ref, g_ref, pall_ref, mask_ref = refs[:6]
    roll_in = refs[6:6 + 2 * nroll]
    outs = refs[6 + (3 if aliased else 2) * nroll:]
    o_ref, sf_ref = outs[:2]
    roll_out = outs[2:2 + nroll]
    s_scr = outs[2 + nroll]
    i = pl.program_id(0)

    @pl.when(i == 0)
    def _():
        s_scr[...] = jnp.zeros_like(s_scr)

    for g in range(nroll):
        blk_ref, halo_ref = roll_in[2 * g], roll_in[2 * g + 1]
        rows, drop = blk_ref.shape[0], halo_ref.shape[0]
        roll_out[g][0:rows - drop, :] = blk_ref[drop:rows, :]
        roll_out[g][rows - drop:rows, :] = halo_ref[...]

    C = GLA_C
    for c in range(q_ref.shape[0] // C):
        rs = slice(c * C, (c + 1) * C)
        for h in range(GLA_HEADS):
            ks = slice(h * GLA_DK, (h + 1) * GLA_DK)
            vs = slice(h * GLA_DV, (h + 1) * GLA_DV)
            v = v_ref[rs, vs]
            o, qe, kdt, bt = _gla_core(q_ref[rs, ks], k_ref[rs, ks], v, g_ref[rs, ks], pall_ref, mask_ref, nl)
            s = s_scr[h]
            o_ref[rs, vs] = (o + _dot(qe, s.astype(BF16))).astype(BF16)
            s_scr[h] = s * jnp.exp(bt[:, C - 1:C]) + _dot(kdt, v)

    @pl.when(i == pl.num_programs(0) - 1)
    def _():
        sf_ref[...] = s_scr[...]


def _gla_prompt(z, la, caches, rolled, layer, drop):
    m = z.shape[0]
    nb = caches[0].shape[1]
    C = GLA_C * GLA_CHUNKS_PER_STEP
    if m % C or (m // C) % nb:
        C = GLA_C
    nsteps = m // C
    nl = GLA_C.bit_length() - 1
    pall, masks = _gla_consts(GLA_C, nl, 0)
    pall = jnp.asarray(pall, BF16)
    masks = jnp.asarray(masks, F32)
    qk = GLA_HEADS * GLA_DK
    va = GLA_HEADS * GLA_DV
    aliased = rolled is not None
    spw = nsteps // nb
    assert spw * nb == nsteps
    roll_in, roll_args, roll_out = [], [], []
    for c in caches:
        wr = c.shape[2]
        rows = wr // spw
        assert rows * spw == wr and rows % drop == 0
        per = rows // drop
        roll_in += [pl.BlockSpec((None, None, rows, LANES), lambda i: (layer, i // spw, i % spw, 0)),
                    pl.BlockSpec((None, None, drop, LANES),
                                 lambda i, per=per, last=wr // drop - 1:
                                 (layer, i // spw, jnp.minimum((i % spw + 1) * per, last), 0))]
        roll_args += [c, c]
        roll_out.append(pl.BlockSpec((None, None, rows, LANES), lambda i: (layer, i // spw, i % spw, 0)))
    anyspec = pl.BlockSpec(memory_space=pl.ANY)
    nroll = len(caches)
    n_in = 6 + 2 * nroll
    outs = pl.pallas_call(
        functools.partial(_gla_prompt_kernel, nl=nl, nroll=nroll, aliased=aliased),
        grid=(nsteps,),
        in_specs=[pl.BlockSpec((C, qk), lambda i: (i, Z_QA)),
                  pl.BlockSpec((C, qk), lambda i: (i, Z_KA)),
                  pl.BlockSpec((C, va), lambda i: (i, Z_VA * CW // va)),
                  pl.BlockSpec((C, qk), lambda i: (i, 0)),
                  _resident(pall.shape), _resident(masks.shape)] + roll_in
        + ([anyspec] * nroll if aliased else []),
        out_specs=[pl.BlockSpec((C, va), lambda i: (i, 0)),
                   pl.BlockSpec((GLA_HEADS, GLA_DK, GLA_DV), lambda i: (0, 0, 0))] + roll_out,
        out_shape=[jax.ShapeDtypeStruct((m, va), BF16),
                   jax.ShapeDtypeStruct((GLA_HEADS, GLA_DK, GLA_DV), F32)]
        + [jax.ShapeDtypeStruct(c.shape, F32) for c in caches],
        scratch_shapes=[pltpu.VMEM((GLA_HEADS, GLA_DK, GLA_DV), F32)],
        input_output_aliases={n_in + g: 2 + g for g in range(nroll)} if aliased else {},
        compiler_params=_params(("arbitrary",)),
        name="gla_prompt",
    )(z, z, z, la, pall, masks, *roll_args, *(rolled if aliased else []))
    return outs[0], outs[1], list(outs[2:])


def _gla_sample_kernel(*refs, nl, tvalid, aliased):
    q_ref, k_ref, v_ref, g_ref, s0_ref, pall_ref, mask_ref = refs[:7]
    o_ref, sf_ref = refs[8:] if aliased else refs[7:]
    C = q_ref.shape[0]
    nseq = C // SROWS
    row = lax.broadcasted_iota(jnp.int32, (C, 1), 0)
    g = jnp.where(row % SROWS < tvalid, g_ref[...], 0.0)
    v = v_ref[...]
    o, qe, kdt, bt = _gla_core(q_ref[...], k_ref[...], v, g, pall_ref, mask_ref, nl)
    zero_v = jnp.zeros_like(v)
    for j in range(nseq):
        mine = row // SROWS == j
        s = s0_ref[j, 0]
        o = o + jnp.where(mine, _dot(qe, s.astype(BF16)), 0.0)
        last = j * SROWS + SROWS - 1
        sf_ref[j, 0] = s * jnp.exp(bt[:, last:last + 1]) + _dot(kdt, jnp.where(mine, v, zero_v))
    o_ref[...] = o.astype(BF16)


def _gla_sample(z, la, states, new_states, layer, tvalid):
    m = z.shape[0]
    C = GLA_C
    nseq = C // SROWS
    nl = (tvalid - 1).bit_length()
    pall, masks = _gla_consts(C, nl, SROWS)
    pall = jnp.asarray(pall, BF16)
    masks = jnp.asarray(masks, F32)
    va = GLA_HEADS * GLA_DV
    sspec = pl.BlockSpec((None, nseq, 1, GLA_DK, GLA_DV), lambda c, h: (layer, c, h, 0, 0))
    aliased = new_states is not None
    return pl.pallas_call(
        functools.partial(_gla_sample_kernel, nl=nl, tvalid=tvalid, aliased=aliased),
        grid=(m // C, GLA_HEADS),
        in_specs=[pl.BlockSpec((C, GLA_DK), lambda c, h: (c, Z_QA * CW // GLA_DK + h)),
                  pl.BlockSpec((C, GLA_DK), lambda c, h: (c, Z_KA * CW // GLA_DK + h)),
                  pl.BlockSpec((C, GLA_DV), lambda c, h: (c, Z_VA * CW // GLA_DV + h)),
                  pl.BlockSpec((C, GLA_DK), lambda c, h: (c, h)),
                  sspec, _resident(pall.shape), _resident(masks.shape)]
        + ([pl.BlockSpec(memory_space=pl.ANY)] if aliased else []),
        out_specs=[pl.BlockSpec((C, GLA_DV), lambda c, h: (c, h)), sspec],
        out_shape=[jax.ShapeDtypeStruct((m, va), BF16),
                   jax.ShapeDtypeStruct(states.shape, F32)],
        input_output_aliases={7: 1} if aliased else {},
        compiler_params=_params(("parallel", "parallel")),
        name="gla_sample",
    )(z, z, z, la, states, pall, masks, *([new_states] if aliased else []))


def _dil_prompt_kernel(q_ref, k_ref, kp_ref, v_ref, vp_ref, o_ref, lse_ref):
    i = pl.program_id(1)
    rb = q_ref.shape[0]
    nk = 2 * DIL_STEPS
    qi = lax.broadcasted_iota(jnp.int32, (DIL_STEPS, nk), 0)
    ki = lax.broadcasted_iota(jnp.int32, (DIL_STEPS, nk), 1)
    lane = lax.broadcasted_iota(jnp.int32, (DIL_STEPS, LANES), 1)
    band = (ki >= qi) & (ki <= qi + DIL_STEPS)
    bias = jnp.where(band, 0.0, NEG)
    bias_first = jnp.where((i > 0) | (ki >= DIL_STEPS), bias, NEG)
    ones = jnp.ones((nk, DIL_HD), BF16)
    for sb in range(rb // DIL_STEPS):
        rows = slice(sb * DIL_STEPS, (sb + 1) * DIL_STEPS)
        lse_tile = jnp.zeros((DIL_STEPS, LANES), F32)
        for h in range(DIL_HEADS):
            hs = slice(h * DIL_HD, (h + 1) * DIL_HD)
            if sb == 0:
                kcat = jnp.concatenate([kp_ref[:, hs], k_ref[rows, hs]], axis=0)
                vcat = jnp.concatenate([vp_ref[:, hs], v_ref[rows, hs]], axis=0)
            else:
                krows = slice((sb - 1) * DIL_STEPS, (sb + 1) * DIL_STEPS)
                kcat, vcat = k_ref[krows, hs], v_ref[krows, hs]
            s = _dot_nt(q_ref[rows, hs], kcat) + (bias_first if sb == 0 else bias)
            mx = jnp.max(s, axis=-1, keepdims=True)
            p = jnp.exp(s - mx).astype(BF16)
            pv = _dot(p, jnp.concatenate([vcat, ones], axis=1))
            den = pv[:, DIL_HD:]
            o_ref[rows, hs] = (pv[:, :DIL_HD] / den).astype(BF16)
            lse_tile = jnp.where(lane == h, mx + jnp.log(den), lse_tile)
        lse_ref[rows, :] = lse_tile


def _dil_prompt(qkv, rb):
    dil, l, _ = qkv.shape
    nprev = rb // DIL_STEPS
    own = lambda c: pl.BlockSpec((None, rb, CW), lambda r, i: (r, i, c))
    prev = lambda c: pl.BlockSpec((None, DIL_STEPS, CW), lambda r, i: (r, jnp.maximum(i * nprev - 1, 0), c))
    return pl.pallas_call(
        _dil_prompt_kernel,
        grid=(dil, l // rb),
        in_specs=[own(0), own(1), prev(1), own(2), prev(2)],
        out_specs=[pl.BlockSpec((None, rb, CW), lambda r, i: (r, i, 0)),
                   pl.BlockSpec((None, rb, LANES), lambda r, i: (r, i, 0))],
        out_shape=[jax.ShapeDtypeStruct((dil, l, CW), BF16),
                   jax.ShapeDtypeStruct((dil, l, LANES), F32)],
        compiler_params=_params(("parallel", "parallel")),
        name=f"dil_prompt_d{dil}",
    )(qkv, qkv, qkv, qkv, qkv)


NUNIT = DEC_T * DIL_HEADS
KEYS = DIL_STEPS * 2 * DIL_HEADS + LANES


def _dil_sample_kernel(q_ref, kvn_ref, c0_ref, c1_ref, c2_ref, a0_ref, a1_ref, a2_ref,
                       o_ref, n0_ref, n1_ref, n2_ref, *, bb, tvalid):
    del a0_ref, a1_ref, a2_ref
    rpt = 2 * DIL_HEADS
    nrow = DIL_STEPS * rpt
    qall = q_ref[...]
    kvn = kvn_ref[...]
    row_u = lax.broadcasted_iota(jnp.int32, (NUNIT, KEYS), 0)
    key = lax.broadcasted_iota(jnp.int32, (NUNIT, KEYS), 1)
    t_u = row_u // DIL_HEADS
    mine = key % rpt == row_u % DIL_HEADS
    tok = jnp.where(key < nrow, key // rpt, (key - nrow) // rpt)
    bias_d1 = jnp.where(mine, jnp.where(key < nrow, jnp.where(tok >= t_u, 0.0, NEG),
                                        jnp.where(tok <= t_u, 0.0, NEG)), NEG)
    bias_dn = jnp.where(mine, jnp.where(key < nrow, 0.0, jnp.where(tok == t_u, 0.0, NEG)), NEG)
    t_row = lax.broadcasted_iota(jnp.int32, (NUNIT, DIL_HD), 0) // DIL_HEADS
    t_col = lax.broadcasted_iota(jnp.int32, (NUNIT, 1), 0) // DIL_HEADS
    c_refs = (c0_ref, c1_ref, c2_ref)
    n_refs = (n0_ref, n1_ref, n2_ref)

    passes = []
    scores = []
    for b in range(bb):
        for g, (_, dil) in enumerate(DIL_PAIRS):
            c_ref = c_refs[g]
            new_rows = []
            for t in range(tvalid):
                r = b * SROWS + t
                blk = jnp.concatenate(
                    [kvn[r:r + 1, g * 2 * CW + j * DIL_HD:g * 2 * CW + (j + 1) * DIL_HD] for j in range(rpt)],
                    axis=0)
                n_refs[g][b, t * rpt:(t + 1) * rpt, :] = blk
                new_rows.append(blk)
            new_rows.append(jnp.zeros((KEYS - nrow - tvalid * rpt, DIL_HD), F32))
            q_rows = jnp.concatenate(
                [qall[b * SROWS + t:b * SROWS + t + 1, g * CW + h * DIL_HD:g * CW + (h + 1) * DIL_HD]
                 for t in range(DEC_T) for h in range(DIL_HEADS)], axis=0)
            for t in (range(DEC_T) if dil > 1 else (None,)):
                rows = c_ref[b, :, 0 if t is None else t].reshape(nrow, DIL_HD)
                kv = jnp.concatenate([rows] + new_rows, axis=0).astype(BF16)
                q = q_rows if t is None else jnp.where(t_row == t, q_rows, 0.0)
                scores.append(_dot_nt(q.astype(BF16), kv) + (bias_d1 if t is None else bias_dn))
                passes.append((b, g, t, kv))
    s = jnp.concatenate(scores, axis=0)
    mx = jnp.max(s, axis=-1, keepdims=True)
    p = jnp.exp(s - mx)
    den = jnp.sum(p, axis=-1, keepdims=True)
    lse = mx + jnp.log(den)
    pv = pltpu.roll(p, DIL_HEADS, axis=1).astype(BF16)

    outs = [[jnp.zeros((NUNIT, DIL_HD), F32)] * N_GROUPS for _ in range(bb)]
    lses = [[jnp.zeros((NUNIT, 1), F32)] * N_GROUPS for _ in range(bb)]
    for i, (b, g, t, kv) in enumerate(passes):
        rs = slice(i * NUNIT, (i + 1) * NUNIT)
        o = _dot(pv[rs], kv) / den[rs]
        if t is None:
            outs[b][g], lses[b][g] = o, lse[rs]
        else:
            outs[b][g] = jnp.where(t_row == t, o, outs[b][g])
            lses[b][g] = jnp.where(t_col == t, lse[rs], lses[b][g])
    for b in range(bb):
        lb, ob = lses[b], outs[b]
        mxl = jnp.maximum(jnp.maximum(lb[0], lb[1]), lb[2])
        es = [jnp.exp(l - mxl) for l in lb]
        inv = 1.0 / (es[0] + es[1] + es[2])
        merged = (es[0] * inv) * ob[0] + (es[1] * inv) * ob[1] + (es[2] * inv) * ob[2]
        orow = [jnp.concatenate([merged[t * DIL_HEADS + h:t * DIL_HEADS + h + 1, :] for h in range(DIL_HEADS)],
                                axis=1) for t in range(DEC_T)]
        o_ref[b * SROWS:(b + 1) * SROWS, :] = jnp.concatenate(
            orow + [jnp.zeros((SROWS - DEC_T, DIL_OUT), F32)], axis=0)


def _dil_sample(qf, kvn, caches, rolled, layer, tvalid, bb):
    assert tvalid == DEC_T
    m = qf.shape[0]
    nb = caches[0].shape[1]
    rows = bb * SROWS
    rpt = 2 * DIL_HEADS
    views, cspecs, nspecs = [], [], []
    for c, (_, dil) in zip(caches, DIL_PAIRS):
        depth, _, wr, _ = c.shape
        views.append(c.reshape(depth, nb, DIL_STEPS, dil, rpt, LANES))
        cspecs.append(pl.BlockSpec((None, bb, DIL_STEPS, min(dil, DEC_T), rpt, LANES),
                                   lambda i: (layer, i, 0, 0, 0, 0)))
        nspecs.append(pl.BlockSpec((None, bb, tvalid * rpt, LANES),
                                   lambda i, last=wr // (tvalid * rpt) - 1: (layer, i, last, 0)))
    anyspec = pl.BlockSpec(memory_space=pl.ANY)
    outs = pl.pallas_call(
        functools.partial(_dil_sample_kernel, bb=bb, tvalid=tvalid),
        grid=(nb // bb,),
        in_specs=[pl.BlockSpec((rows, N_GROUPS * CW), lambda i: (i, 0)),
                  pl.BlockSpec((rows, N_GROUPS * 2 * CW), lambda i: (i, 0))] + cspecs + [anyspec] * N_GROUPS,
        out_specs=[pl.BlockSpec((rows, CW), lambda i: (i, 0))] + nspecs,
        out_shape=[jax.ShapeDtypeStruct((m, CW), F32)] + [jax.ShapeDtypeStruct(r.shape, F32) for r in rolled],
        input_output_aliases={2 + N_GROUPS + g: 1 + g for g in range(N_GROUPS)},
        compiler_params=_params(("parallel",)),
        name="dil_sample",
    )(qf, kvn, *views, *rolled)
    return outs[0], list(outs[1:])


def _tokens_from_streams(ref, scr):
    dil, n, w = ref.shape
    if dil == 1:
        return ref[0].astype(F32)
    nslab = w // LANES
    for r in range(dil):
        blk = ref[r].astype(F32)
        for j in range(nslab):
            scr[j, pl.ds(r, n, stride=dil), :] = blk[:, j * LANES:(j + 1) * LANES]
    return jnp.concatenate([scr[j] for j in range(nslab)], axis=1)


def _merge_kernel(*refs, grouped):
    if grouped:
        (x_ref, oa_ref, r_ref, ga_ref, gb_ref, o0_ref, o1_ref, o2_ref, l0_ref, l1_ref, l2_ref,
         gn_ref, wa_ref, wb_ref, wo_ref, nw_ref, h_ref, o_scr, l_scr) = refs
    else:
        x_ref, oa_ref, r_ref, ga_ref, gb_ref, ob_ref, gn_ref, wa_ref, wb_ref, wo_ref, nw_ref, h_ref = refs
    oa = oa_ref[...].astype(F32)
    gn = gn_ref[...]
    parts = [_rms(oa[:, h * GLA_DV:(h + 1) * GLA_DV], gn) for h in range(GLA_HEADS)]
    r = r_ref[...].astype(F32)
    oan = (jnp.concatenate(parts, axis=1) * (r * jax.nn.sigmoid(r))).astype(BF16)

    if grouped:
        lses = [_tokens_from_streams(l, l_scr) for l in (l0_ref, l1_ref, l2_ref)]
        mx = jnp.maximum(jnp.maximum(lses[0], lses[1]), lses[2])
        es = [jnp.exp(l - mx) for l in lses]
        inv = 1.0 / (es[0] + es[1] + es[2])
        outs = [_tokens_from_streams(o, o_scr) for o in (o0_ref, o1_ref, o2_ref)]
        ob = []
        for h in range(DIL_HEADS):
            hs = slice(h * DIL_HD, (h + 1) * DIL_HD)
            acc = None
            for e, o in zip(es, outs):
                term = (e * inv)[:, h:h + 1] * o[:, hs]
                acc = term if acc is None else acc + term
            ob.append(acc)
        ob = jnp.concatenate(ob, axis=1).astype(BF16)
    else:
        ob = ob_ref[...].astype(BF16)

    merged = (ga_ref[...].astype(F32) * _dot(oan, wa_ref[...])
              + gb_ref[...].astype(F32) * _dot(ob, wb_ref[...]))
    y = _dot(merged.astype(BF16), wo_ref[...])
    h_ref[...] = x_ref[...] + _rms(y, nw_ref[...])


def _merge(x, oa, z, dil_o, dil_lse, gn, wa, wb, wo, nw, tm):
    m, d = x.shape
    va = oa.shape[1]
    grouped = dil_lse is not None
    zc = lambda c: pl.BlockSpec((tm, d), lambda i: (i, c * CW // d))
    row = lambda w: pl.BlockSpec((tm, w), lambda i: (i, 0))
    stream = lambda a: pl.BlockSpec((a.shape[0], tm // a.shape[0], a.shape[2]), lambda i: (0, i, 0))
    if grouped:
        dil_specs = [stream(a) for a in dil_o] + [stream(a) for a in dil_lse]
        dil_args = [*dil_o, *dil_lse]
        scratch = [pltpu.VMEM((CW // LANES, tm, LANES), F32), pltpu.VMEM((1, tm, LANES), F32)]
    else:
        dil_specs, dil_args, scratch = [row(CW)], [dil_o], []
    return pl.pallas_call(
        functools.partial(_merge_kernel, grouped=grouped),
        grid=(m // tm,),
        in_specs=[row(d), row(va), zc(Z_RA), zc(Z_GA), zc(Z_GB)] + dil_specs
        + [_resident(gn.shape), _resident(wa.shape), _resident(wb.shape),
           _resident(wo.shape), _resident(nw.shape)],
        out_specs=row(d),
        out_shape=jax.ShapeDtypeStruct((m, d), F32),
        scratch_shapes=scratch,
        compiler_params=_params(("parallel",)),
        name="merge",
    )(x, oa, z, z, z, *dil_args, gn, wa, wb, wo, nw)


def _gelu_tanh(x):
    return 0.5 * x * (1.0 + jnp.tanh(0.7978845608028654 * (x + 0.044715 * x * x * x)))


def _ffn_kernel(*refs, fc, sample):
    if sample:
        h_ref, nw1_ref, wup_ref, wc_ref, bc_ref, wdn_ref, nw2_ref, p1_ref, p2_ref, out_ref, gst_ref = refs
    else:
        h_ref, nw1_ref, wup_ref, wc_ref, bc_ref, wdn_ref, nw2_ref, out_ref, gst_ref, cbuf = refs
    tm = h_ref.shape[0]
    dff = wdn_ref.shape[0]
    i = pl.program_id(0)

    if not sample:
        @pl.when(i == 0)
        def _():
            cbuf[...] = jnp.zeros_like(cbuf)

    h = h_ref[...]
    hn = _rms(h, nw1_ref[...]).astype(BF16)
    if sample:
        tpos = lax.broadcasted_iota(jnp.int32, (tm, 1), 0) % SROWS
    else:
        row8 = lax.broadcasted_iota(jnp.int32, (SUBLANES, 1), 0)
    acc = None
    for c in range(dff // fc):
        cs = slice(c * fc, (c + 1) * fc)
        gp = _dot(hn, wup_ref[:, cs])
        val = _dot(hn, wup_ref[:, dff + c * fc:dff + (c + 1) * fc])
        g1 = pltpu.roll(gp, 1, axis=0)
        g2 = pltpu.roll(gp, 2, axis=0)
        if sample:
            g1 = jnp.where(tpos == 0, p1_ref[:, cs], g1)
            g2 = jnp.where(tpos < 2, p2_ref[:, cs], g2)
            gst_ref[:, cs] = gp
        else:
            carry = cbuf[:, cs]
            head1 = jnp.where(row8 < 1, pltpu.roll(carry, 1, axis=0), g1[0:SUBLANES])
            head2 = jnp.where(row8 < 2, pltpu.roll(carry, 2, axis=0), g2[0:SUBLANES])
            g1 = jnp.concatenate([head1, g1[SUBLANES:]], axis=0)
            g2 = jnp.concatenate([head2, g2[SUBLANES:]], axis=0)
            cbuf[:, cs] = gp[tm - SUBLANES:]
        conv = g2 * wc_ref[0:1, cs] + g1 * wc_ref[1:2, cs] + gp * wc_ref[2:3, cs] + bc_ref[:, cs]
        act = (_gelu_tanh(conv) * val).astype(BF16)
        part = _dot(act, wdn_ref[cs, :])
        acc = part if acc is None else acc + part
    if not sample:
        gst_ref[...] = cbuf[...]
    out_ref[...] = h + _rms(acc, nw2_ref[...])


def _ffn(h, nw1, wup, wc, bc, wdn, nw2, prev, tm, fc):
    m, d = h.shape
    dff = wdn.shape[0]
    sample = prev is not None
    row = lambda w: pl.BlockSpec((tm, w), lambda i: (i, 0))
    in_specs = [row(d), _resident((1, d)), _resident(wup.shape), _resident(wc.shape),
                _resident((1, dff)), _resident(wdn.shape), _resident((1, d))]
    args = [h, nw1, wup, wc, bc, wdn, nw2]
    out_specs, out_shape, scratch = [row(d)], [jax.ShapeDtypeStruct((m, d), F32)], []
    if sample:
        in_specs += [row(dff), row(dff)]
        args += list(prev)
        out_specs.append(row(dff))
        out_shape.append(jax.ShapeDtypeStruct((m, dff), F32))
    else:
        out_specs.append(pl.BlockSpec((SUBLANES, dff), lambda i: (i, 0)))
        out_shape.append(jax.ShapeDtypeStruct((m // tm * SUBLANES, dff), F32))
        scratch.append(pltpu.VMEM((SUBLANES, dff), F32))
    return pl.pallas_call(
        functools.partial(_ffn_kernel, fc=fc, sample=sample),
        grid=(m // tm,),
        in_specs=in_specs,
        out_specs=out_specs,
        out_shape=out_shape,
        scratch_shapes=scratch,
        compiler_params=_params(("arbitrary",)),
        name="ffn_sample" if sample else "ffn_prompt",
    )(*args)


def _rope_tables(pos):
    inv = ROPE_THETA ** (-jnp.arange(ROPE_HALF, dtype=F32) * 2.0 / ROPE_DIM)
    ang = pos[:, None] * inv[None, :]
    cos, sin = jnp.cos(ang), jnp.sin(ang)
    pad = LANES - ROPE_DIM
    one = jnp.ones((pos.shape[0], pad), F32)
    cos_t = jnp.concatenate([cos, cos, one], axis=1)
    sin_t = jnp.concatenate([-sin, sin, 0.0 * one], axis=1)
    return cos_t, sin_t


def _prep_in(w_in_l, d):
    qk = GLA_HEADS * GLA_DK
    va = GLA_HEADS * GLA_DV
    nb = N_GROUPS * DIL_OUT
    sizes = (qk, qk, va, va, GLA_RANK, 3 * nb, 2 * d)
    offs = np.cumsum((0,) + sizes)
    q_a, k_a, v_a, r_a, a_low, qkv_b, gate = (w_in_l[:, offs[j]:offs[j + 1]] for j in range(7))
    q_b, k_b, v_b = qkv_b[:, :nb], qkv_b[:, nb:2 * nb], qkv_b[:, 2 * nb:]
    wz = jnp.concatenate([q_a, k_a, v_a, r_a, gate, q_b, k_b, v_b], axis=1).astype(BF16)
    wal = jnp.pad(a_low, ((0, 0), (0, LANES - GLA_RANK))).astype(BF16)
    kv = []
    for g in range(N_GROUPS):
        kv += [k_b[:, g * DIL_OUT:(g + 1) * DIL_OUT], v_b[:, g * DIL_OUT:(g + 1) * DIL_OUT]]
    wkv = jnp.concatenate(kv, axis=1).astype(BF16)
    return wz, wal, wkv


def kernel(x_prompt, x_sample, state_gla, cache_dil_w128, cache_dil_w512, cache_dil_w2048, state_ffn_conv,
           w_in, w_gate_up, b_gate, gla_norm, w_branch_a, w_branch_b, w_out, norm_pre_mix, norm_post_mix,
           norm_pre_ffn, norm_post_ffn, w_ffn_up, w_conv, b_conv, w_ffn_down):
    depth, d = w_in.shape[0], w_in.shape[1]
    _, t_p, _ = x_prompt.shape
    nb, t_s, _ = x_sample.shape
    dff = w_ffn_down.shape[1]
    caches = (cache_dil_w128, cache_dil_w512, cache_dil_w2048)
    assert x_prompt.shape[0] == 1 and t_s <= SROWS // 2 and d == 2 * DIL_OUT
    for c, (win, _) in zip(caches, DIL_PAIRS):
        assert c.shape[2] == win and t_p >= win
    dils = tuple(dl for _, dl in DIL_PAIRS)
    tm = 256
    tm_big = min(512, t_p)
    m_s = nb * SROWS
    tm_s = min(tm, m_s)
    tm_ffn_s = min(128, m_s)

    xp = x_prompt.reshape(t_p, d)
    xs = jnp.pad(x_sample, ((0, 0), (0, SROWS - t_s), (0, 0))).reshape(nb * SROWS, d)
    cos_p, sin_p = _rope_tables(jnp.arange(t_p, dtype=F32))
    pos_s = PAST_LEN + jnp.minimum(jnp.arange(SROWS), t_s - 1).astype(F32)
    cos_s, sin_s = (jnp.tile(a, (nb, 1)) for a in _rope_tables(pos_s))
    cache_v = [c.reshape(depth, nb, c.shape[2] * 2 * DIL_HEADS, DIL_HD) for c in caches]
    new_cache = None

    p_gla, p_conv, s_conv = [], [], []
    p_kv = s_gla = None
    wins = tuple(w for w, _ in DIL_PAIRS)
    row2 = lambda a: a.reshape(1, -1)
    for i in range(depth):
        wz, wal, wkv = _prep_in(w_in[i], d)
        wgu = jnp.pad(w_gate_up[i], ((0, LANES - GLA_RANK), (0, 0))).astype(BF16)
        bg = row2(b_gate[i])
        wa, wb, wo = (w.astype(BF16) for w in (w_branch_a[i], w_branch_b[i], w_out[i]))
        wup, wdn = w_ffn_up[i].astype(BF16), w_ffn_down[i].astype(BF16)
        n1, n2, n3, n4 = (row2(n[i]) for n in (norm_pre_mix, norm_post_mix, norm_pre_ffn, norm_post_ffn))
        gn = row2(gla_norm[i])
        wc = jnp.pad(w_conv[i], ((0, SUBLANES - CONV_W), (0, 0)))
        bc = row2(b_conv[i])

        z, la, *qkv = _inproj(xp, n1, wz, wal, wgu, bg, cos_p, sin_p, tm_big, dils)
        p_kv = _kvtail_prompt(xp, n1, wkv, cos_p, sin_p, tm, wins, p_kv, i, depth)
        oa, sfin, new_cache = _gla_prompt(z, la, cache_v, new_cache, i, t_s * 2 * DIL_HEADS)
        dil = [_dil_prompt(a, min(DIL_ROWS_PER_STEP, a.shape[1])) for a in qkv]
        h = _merge(xp, oa, z, [o for o, _ in dil], [l for _, l in dil], gn, wa, wb, wo, n2, tm_big)
        xp, gst = _ffn(h, n3, wup, wc, bc, wdn, n4, None, tm_big, 1024)
        p_gla.append(sfin[None])
        p_conv.append(gst[-(CONV_W - 1):][None])

        z, la, *qkv = _inproj(xs, n1, wz, wal, wgu, bg, cos_s, sin_s, tm_s, (1,) * N_GROUPS)
        kvn = _kvtail(xs, n1, wkv, cos_s, sin_s, tm_s)
        oa, s_gla = _gla_sample(z, la, state_gla, s_gla, i, t_s)
        qf = jnp.concatenate([a[0, :, :CW] for a in qkv], axis=1).astype(F32)
        ob, new_cache = _dil_sample(qf, kvn, cache_v, new_cache, i, t_s, min(2, nb))
        h = _merge(xs, oa, z, ob, None, gn, wa, wb, wo, n2, tm_s)
        st = state_ffn_conv[i]
        zpad = lambda a: jnp.pad(a, ((0, 0), (0, SROWS - a.shape[1]), (0, 0))).reshape(m_s, dff)
        prev = (zpad(st[:, 1:2]), zpad(st))
        xs, gpre = _ffn(h, n3, wup, wc, bc, wdn, n4, prev, tm_ffn_s, 512)
        s_conv.append(gpre.reshape(nb, SROWS, dff)[:, t_s - (CONV_W - 1):t_s])

    y_p = xp.reshape(1, t_p, d)
    y_s = xs.reshape(nb, SROWS, d)[:, :t_s]
    p_caches = [kv.reshape(depth, 1, win, 2, DIL_HEADS, DIL_HD) for kv, win in zip(p_kv, wins)]
    s_caches = [c.reshape(o.shape) for c, o in zip(new_cache, caches)]
    return (y_p, y_s, jnp.stack(p_gla), *p_caches, jnp.stack(p_conv),
            s_gla, *s_caches, jnp.stack(s_conv))
```

```python
import functools

import numpy as np
import jax
import jax.numpy as jnp
from jax import lax
from jax.experimental import pallas as pl
from jax.experimental.pallas import tpu as pltpu

F32 = jnp.float32
BF16 = jnp.bfloat16

GLA_HEADS = 4
GLA_DK = 128
GLA_DV = 256
GLA_RANK = 16
GLA_TAU = 16.0
DIL_PAIRS = ((128, 1), (512, 4), (2048, 16))
N_GROUPS = len(DIL_PAIRS)
DIL_HEADS = 4
DIL_HD = 128
DIL_STEPS = 128
DIL_OUT = DIL_HEADS * DIL_HD
ROPE_THETA = 500000.0
ROPE_DIM = DIL_HD // 4
ROPE_HALF = ROPE_DIM // 2
CONV_W = 3
EPS = 1e-6
PAST_LEN = 16384

LANES = 128
SUBLANES = 8
VMEM_LIMIT = 56 * 1024 * 1024

CW = 512
Z_QA, Z_KA, Z_VA, Z_RA, Z_GA, Z_GB, Z_QB = 0, 1, 2, 4, 6, 8, 10
Z_BLOCKS = 19
SROWS = 8
DEC_T = 4
GLA_C = 128
GLA_CHUNKS_PER_STEP = 4
DIL_ROWS_PER_STEP = 2048
NEG = -1e30


def _params(sem):
    return pltpu.CompilerParams(dimension_semantics=sem, vmem_limit_bytes=VMEM_LIMIT)


def _resident(shape):
    nd = len(shape)
    return pl.BlockSpec(shape, lambda *_: (0,) * nd, pipeline_mode=pl.Buffered(1))


def _rms(x, w):
    ms = jnp.mean(x * x, axis=-1, keepdims=True)
    return x * lax.rsqrt(ms + EPS) * w


def _dot(a, b):
    return jnp.dot(a, b, preferred_element_type=F32)


def _dot_nt(a, b):
    return lax.dot_general(a, b, (((1,), (1,)), ((), ())), preferred_element_type=F32)


def _rope_heads(acc, cos_t, sin_t, lt):
    parts = []
    for h in range(DIL_HEADS):
        a = acc[:, h * LANES:(h + 1) * LANES]
        up = pltpu.roll(a, LANES - ROPE_HALF, axis=1)
        dn = pltpu.roll(a, ROPE_HALF, axis=1)
        parts.append(a * cos_t + jnp.where(lt, up, dn) * sin_t)
    return parts


def _inproj_kernel(x_ref, nw_ref, wz_ref, wal_ref, wgu_ref, bg_ref, cos_ref, sin_ref,
                   z_ref, la_ref, g0_ref, g1_ref, g2_ref, scr, *, dils):
    tm = x_ref.shape[0]
    xn = _rms(x_ref[...], nw_ref[...]).astype(BF16)
    cos_t = cos_ref[...]
    sin_t = sin_ref[...]
    lt = lax.broadcasted_iota(jnp.int32, cos_t.shape, 1) < ROPE_HALF
    grefs = (g0_ref, g1_ref, g2_ref)
    for c in range(Z_BLOCKS):
        acc = _dot(xn, wz_ref[:, c * CW:(c + 1) * CW])
        if c < Z_QB:
            if c == Z_QA:
                acc = acc * (GLA_DK ** -0.5)
            elif c >= Z_GA:
                acc = jax.nn.sigmoid(acc)
            z_ref[:, c * CW:(c + 1) * CW] = acc.astype(BF16)
            continue
        part, g = divmod(c - Z_QB, N_GROUPS)
        if part < 2:
            acc = jnp.concatenate(_rope_heads(acc, cos_t, sin_t, lt), axis=1)
        if part == 0:
            acc = acc * (DIL_HD ** -0.5)
        cols = slice(part * CW, (part + 1) * CW)
        dil = dils[g]
        if dil == 1:
            grefs[g][0, :, cols] = acc.astype(BF16)
        else:
            for j in range(CW // LANES):
                scr[j] = acc[:, j * LANES:(j + 1) * LANES]
            for r in range(dil):
                rows = [scr[j, pl.ds(r, tm // dil, stride=dil), :] for j in range(CW // LANES)]
                grefs[g][r, :, cols] = jnp.concatenate(rows, axis=1).astype(BF16)
    al = _dot(xn, wal_ref[...]).astype(BF16)
    lg = _dot(al, wgu_ref[...]) + bg_ref[...]
    la_ref[...] = (jnp.minimum(lg, 0.0) - jnp.log1p(jnp.exp(-jnp.abs(lg)))) * (1.0 / GLA_TAU)


def _inproj(x, nw, wz, wal, wgu, bg, cos_t, sin_t, tm, dils):
    m, d = x.shape
    nw_all = wz.shape[1]
    nz = Z_QB * CW
    qk = wgu.shape[1]
    row = lambda w: pl.BlockSpec((tm, w), lambda i: (i, 0))
    gspec = lambda dl: pl.BlockSpec((dl, tm // dl, 3 * CW), lambda i: (0, i, 0))
    return pl.pallas_call(
        functools.partial(_inproj_kernel, dils=dils),
        grid=(m // tm,),
        in_specs=[row(d), _resident((1, d)), _resident((d, nw_all)), _resident((d, LANES)),
                  _resident((LANES, qk)), _resident((1, qk)), row(LANES), row(LANES)],
        out_specs=[row(nz), row(qk)] + [gspec(dl) for dl in dils],
        out_shape=[jax.ShapeDtypeStruct((m, nz), BF16), jax.ShapeDtypeStruct((m, qk), F32)]
        + [jax.ShapeDtypeStruct((dl, m // dl, 3 * CW), BF16) for dl in dils],
        scratch_shapes=[pltpu.VMEM((CW // LANES, tm, LANES), F32)],
        compiler_params=_params(("parallel",)),
        name="inproj",
    )(x, nw, wz, wal, wgu, bg, cos_t, sin_t)


def _kvtail_kernel(x_ref, nw_ref, w_ref, cos_ref, sin_ref, o_ref):
    xn = _rms(x_ref[...], nw_ref[...]).astype(BF16)
    cos_t = cos_ref[...]
    sin_t = sin_ref[...]
    lt = lax.broadcasted_iota(jnp.int32, cos_t.shape, 1) < ROPE_HALF
    for c in range(2 * N_GROUPS):
        acc = _dot(xn, w_ref[:, c * CW:(c + 1) * CW])
        if c % 2 == 0:
            for h, p in enumerate(_rope_heads(acc, cos_t, sin_t, lt)):
                o_ref[:, c * CW + h * LANES:c * CW + (h + 1) * LANES] = p
        else:
            o_ref[:, c * CW:(c + 1) * CW] = acc


def _kvtail_prompt_kernel(*refs, wins, aliased):
    x_ref, nw_ref, w_ref, cos_ref, sin_ref = refs[:5]
    o_refs = refs[5 + (len(wins) if aliased else 0):]
    tm = x_ref.shape[0]
    i = pl.program_id(0)
    ntile = pl.num_programs(0)
    xn = _rms(x_ref[...], nw_ref[...]).astype(BF16)
    cos_t = cos_ref[...]
    sin_t = sin_ref[...]
    lt = lax.broadcasted_iota(jnp.int32, cos_t.shape, 1) < ROPE_HALF
    rpt = 2 * DIL_HEADS
    for g, win in enumerate(wins):
        take = min(win, tm)

        @pl.when(i >= ntile - max(win // tm, 1))
        def _(g=g, take=take):
            k = jnp.concatenate(_rope_heads(_dot(xn, w_ref[:, 2 * g * CW:(2 * g + 1) * CW]), cos_t, sin_t, lt), axis=1)
            v = _dot(xn, w_ref[:, (2 * g + 1) * CW:(2 * g + 2) * CW])
            for j in range(rpt):
                src = k if j < DIL_HEADS else v
                hs = slice((j % DIL_HEADS) * DIL_HD, (j % DIL_HEADS + 1) * DIL_HD)
                o_refs[g][pl.ds(j, take, stride=rpt), :] = src[tm - take:, hs]


def _kvtail_prompt(x, nw, wkv, cos_t, sin_t, tm, wins, outs, layer, depth):
    t, d = x.shape
    n = wkv.shape[1]
    tail = max(wins)
    ntile = tail // tm
    off = (t - tail) // tm
    rpt = 2 * DIL_HEADS
    row = lambda w: pl.BlockSpec((tm, w), lambda i: (i + off, 0))
    ospecs = []
    for win in wins:
        take = min(win, tm)
        first = ntile - max(win // tm, 1)
        ospecs.append(pl.BlockSpec((None, take * rpt, LANES),
                                   lambda i, first=first: (layer, jnp.maximum(i - first, 0), 0)))
    aliased = outs is not None
    anyspec = pl.BlockSpec(memory_space=pl.ANY)
    return pl.pallas_call(
        functools.partial(_kvtail_prompt_kernel, wins=wins, aliased=aliased),
        grid=(ntile,),
        in_specs=[row(d), _resident((1, d)), _resident((d, n)), row(LANES), row(LANES)]
        + ([anyspec] * len(wins) if aliased else []),
        out_specs=ospecs,
        out_shape=[jax.ShapeDtypeStruct((depth, win * rpt, LANES), F32) for win in wins],
        input_output_aliases={5 + g: g for g in range(len(wins))} if aliased else {},
        compiler_params=_params(("arbitrary",)),
        name="kvtail_prompt",
    )(x, nw, wkv, cos_t, sin_t, *(outs if aliased else []))


def _kvtail(x, nw, wkv, cos_t, sin_t, tm):
    m, d = x.shape
    n = wkv.shape[1]
    row = lambda w: pl.BlockSpec((tm, w), lambda i: (i, 0))
    return pl.pallas_call(
        _kvtail_kernel,
        grid=(m // tm,),
        in_specs=[row(d), _resident((1, d)), _resident((d, n)), row(LANES), row(LANES)],
        out_specs=row(n),
        out_shape=jax.ShapeDtypeStruct((m, n), F32),
        compiler_params=_params(("parallel",)),
        name="kvtail",
    )(x, nw, wkv, cos_t, sin_t)


@functools.lru_cache(maxsize=None)
def _gla_consts(C, nl, seg):
    x = np.arange(C)[:, None]
    i = np.arange(C)[None, :]
    same = (x // seg == i // seg) if seg else np.ones((C, C), bool)
    mats = [(i <= x) & same, (i > x) & same]
    masks = [x == i]
    for l in range(nl):
        m = 1 << l
        r = (x | (2 * m - 1)) - m
        mats.append((i > np.minimum(x, r)) & (i <= np.maximum(x, r)))
        masks.append((((x >> l) ^ (i >> l)) == 1) & (((x >> l) & 1) == 1))
    pall = np.concatenate(mats, 0).astype(np.float32)
    return pall, np.stack(masks).astype(np.float32)


def _gla_core(q, k, v, g, pall_ref, mask_ref, nl):
    C = q.shape[0]
    g_hi = g.astype(BF16)
    g_lo = (g - g_hi.astype(F32)).astype(BF16)
    ex = _dot(pall_ref[...], jnp.concatenate([g_hi, g_lo], axis=1))
    ex = ex[:, :GLA_DK] + ex[:, GLA_DK:]
    b = ex[0:C]
    aft = ex[C:2 * C]
    qf = q.astype(F32)
    kf = k.astype(F32)
    a = _dot_nt(q, k) * mask_ref[0]
    for l in range(nl):
        e = jnp.exp(ex[(2 + l) * C:(3 + l) * C])
        a = a + _dot_nt((qf * e).astype(BF16), (kf * e).astype(BF16)) * mask_ref[1 + l]
    o = _dot(a.astype(BF16), v)
    qe = (qf * jnp.exp(b)).astype(BF16)
    kdt = (kf * jnp.exp(aft)).T.astype(BF16)
    return o, qe, kdt, b.T


def _gla_prompt_kernel(*refs, nl, nroll, aliased):
    q_ref, k_ref, v_ref, g_ref, pall_ref, mask_ref = refs[:6]
    roll_in = refs[6:6 + 2 * nroll]
    outs = refs[6 + (3 if aliased else 2) * nroll:]
    o_ref, sf_ref = outs[:2]
    roll_out = outs[2:2 + nroll]
    s_scr = outs[2 + nroll]
    i = pl.program_id(0)

    @pl.when(i == 0)
    def _():
        s_scr[...] = jnp.zeros_like(s_scr)

    for g in range(nroll):
        blk_ref, halo_ref = roll_in[2 * g], roll_in[2 * g + 1]
        rows, drop = blk_ref.shape[0], halo_ref.shape[0]
        roll_out[g][0:rows - drop, :] = blk_ref[drop:rows, :]
        roll_out[g][rows - drop:rows, :] = halo_ref[...]

    C = GLA_C
    for c in range(q_ref.shape[0] // C):
        rs = slice(c * C, (c + 1) * C)
        for h in range(GLA_HEADS):
            ks = slice(h * GLA_DK, (h + 1) * GLA_DK)
            vs = slice(h * GLA_DV, (h + 1) * GLA_DV)
            v = v_ref[rs, vs]
            o, qe, kdt, bt = _gla_core(q_ref[rs, ks], k_ref[rs, ks], v, g_ref[rs, ks], pall_ref, mask_ref, nl)
            s = s_scr[h]
            o_ref[rs, vs] = (o + _dot(qe, s.astype(BF16))).astype(BF16)
            s_scr[h] = s * jnp.exp(bt[:, C - 1:C]) + _dot(kdt, v)

    @pl.when(i == pl.num_programs(0) - 1)
    def _():
        sf_ref[...] = s_scr[...]


def _gla_prompt(z, la, caches, rolled, layer, drop):
    m = z.shape[0]
    nb = caches[0].shape[1]
    C = GLA_C * GLA_CHUNKS_PER_STEP
    if m % C or (m // C) % nb:
        C = GLA_C
    nsteps = m // C
    nl = GLA_C.bit_length() - 1
    pall, masks = _gla_consts(GLA_C, nl, 0)
    pall = jnp.asarray(pall, BF16)
    masks = jnp.asarray(masks, F32)
    qk = GLA_HEADS * GLA_DK
    va = GLA_HEADS * GLA_DV
    aliased = rolled is not None
    spw = nsteps // nb
    assert spw * nb == nsteps
    roll_in, roll_args, roll_out = [], [], []
    for c in caches:
        wr = c.shape[2]
        rows = wr // spw
        assert rows * spw == wr and rows % drop == 0
        per = rows // drop
        roll_in += [pl.BlockSpec((None, None, rows, LANES), lambda i: (layer, i // spw, i % spw, 0)),
                    pl.BlockSpec((None, None, drop, LANES),
                                 lambda i, per=per, last=wr // drop - 1:
                                 (layer, i // spw, jnp.minimum((i % spw + 1) * per, last), 0))]
        roll_args += [c, c]
        roll_out.append(pl.BlockSpec((None, None, rows, LANES), lambda i: (layer, i // spw, i % spw, 0)))
    anyspec = pl.BlockSpec(memory_space=pl.ANY)
    nroll = len(caches)
    n_in = 6 + 2 * nroll
    outs = pl.pallas_call(
        functools.partial(_gla_prompt_kernel, nl=nl, nroll=nroll, aliased=aliased),
        grid=(nsteps,),
        in_specs=[pl.BlockSpec((C, qk), lambda i: (i, Z_QA)),
                  pl.BlockSpec((C, qk), lambda i: (i, Z_KA)),
                  pl.BlockSpec((C, va), lambda i: (i, Z_VA * CW // va)),
                  pl.BlockSpec((C, qk), lambda i: (i, 0)),
                  _resident(pall.shape), _resident(masks.shape)] + roll_in
        + ([anyspec] * nroll if aliased else []),
        out_specs=[pl.BlockSpec((C, va), lambda i: (i, 0)),
                   pl.BlockSpec((GLA_HEADS, GLA_DK, GLA_DV), lambda i: (0, 0, 0))] + roll_out,
        out_shape=[jax.ShapeDtypeStruct((m, va), BF16),
                   jax.ShapeDtypeStruct((GLA_HEADS, GLA_DK, GLA_DV), F32)]
        + [jax.ShapeDtypeStruct(c.shape, F32) for c in caches],
        scratch_shapes=[pltpu.VMEM((GLA_HEADS, GLA_DK, GLA_DV), F32)],
        input_output_aliases={n_in + g: 2 + g for g in range(nroll)} if aliased else {},
        compiler_params=_params(("arbitrary",)),
        name="gla_prompt",
    )(z, z, z, la, pall, masks, *roll_args, *(rolled if aliased else []))
    return outs[0], outs[1], list(outs[2:])


def _gla_sample_kernel(*refs, nl, tvalid, aliased):
    q_ref, k_ref, v_ref, g_ref, s0_ref, pall_ref, mask_ref = refs[:7]
    o_ref, sf_ref = refs[8:] if aliased else refs[7:]
    C = q_ref.shape[0]
    nseq = C // SROWS
    row = lax.broadcasted_iota(jnp.int32, (C, 1), 0)
    g = jnp.where(row % SROWS < tvalid, g_ref[...], 0.0)
    v = v_ref[...]
    o, qe, kdt, bt = _gla_core(q_ref[...], k_ref[...], v, g, pall_ref, mask_ref, nl)
    zero_v = jnp.zeros_like(v)
    for j in range(nseq):
        mine = row // SROWS == j
        s = s0_ref[j, 0]
        o = o + jnp.where(mine, _dot(qe, s.astype(BF16)), 0.0)
        last = j * SROWS + SROWS - 1
        sf_ref[j, 0] = s * jnp.exp(bt[:, last:last + 1]) + _dot(kdt, jnp.where(mine, v, zero_v))
    o_ref[...] = o.astype(BF16)


def _gla_sample(z, la, states, new_states, layer, tvalid):
    m = z.shape[0]
    C = GLA_C
    nseq = C // SROWS
    nl = (tvalid - 1).bit_length()
    pall, masks = _gla_consts(C, nl, SROWS)
    pall = jnp.asarray(pall, BF16)
    masks = jnp.asarray(masks, F32)
    va = GLA_HEADS * GLA_DV
    sspec = pl.BlockSpec((None, nseq, 1, GLA_DK, GLA_DV), lambda c, h: (layer, c, h, 0, 0))
    aliased = new_states is not None
    return pl.pallas_call(
        functools.partial(_gla_sample_kernel, nl=nl, tvalid=tvalid, aliased=aliased),
        grid=(m // C, GLA_HEADS),
        in_specs=[pl.BlockSpec((C, GLA_DK), lambda c, h: (c, Z_QA * CW // GLA_DK + h)),
                  pl.BlockSpec((C, GLA_DK), lambda c, h: (c, Z_KA * CW // GLA_DK + h)),
                  pl.BlockSpec((C, GLA_DV), lambda c, h: (c, Z_VA * CW // GLA_DV + h)),
                  pl.BlockSpec((C, GLA_DK), lambda c, h: (c, h)),
                  sspec, _resident(pall.shape), _resident(masks.shape)]
        + ([pl.BlockSpec(memory_space=pl.ANY)] if aliased else []),
        out_specs=[pl.BlockSpec((C, GLA_DV), lambda c, h: (c, h)), sspec],
        out_shape=[jax.ShapeDtypeStruct((m, va), BF16),
                   jax.ShapeDtypeStruct(states.shape, F32)],
        input_output_aliases={7: 1} if aliased else {},
        compiler_params=_params(("parallel", "parallel")),
        name="gla_sample",
    )(z, z, z, la, states, pall, masks, *([new_states] if aliased else []))


def _dil_prompt_kernel(q_ref, k_ref, kp_ref, v_ref, vp_ref, o_ref, lse_ref):
    i = pl.program_id(1)
    rb = q_ref.shape[0]
    nk = 2 * DIL_STEPS
    qi = lax.broadcasted_iota(jnp.int32, (DIL_STEPS, nk), 0)
    ki = lax.broadcasted_iota(jnp.int32, (DIL_STEPS, nk), 1)
    lane = lax.broadcasted_iota(jnp.int32, (DIL_STEPS, LANES), 1)
    band = (ki >= qi) & (ki <= qi + DIL_STEPS)
    bias = jnp.where(band, 0.0, NEG)
    bias_first = jnp.where((i > 0) | (ki >= DIL_STEPS), bias, NEG)
    ones = jnp.ones((nk, DIL_HD), BF16)
    for sb in range(rb // DIL_STEPS):
        rows = slice(sb * DIL_STEPS, (sb + 1) * DIL_STEPS)
        lse_tile = jnp.zeros((DIL_STEPS, LANES), F32)
        for h in range(DIL_HEADS):
            hs = slice(h * DIL_HD, (h + 1) * DIL_HD)
            if sb == 0:
                kcat = jnp.concatenate([kp_ref[:, hs], k_ref[rows, hs]], axis=0)
                vcat = jnp.concatenate([vp_ref[:, hs], v_ref[rows, hs]], axis=0)
            else:
                krows = slice((sb - 1) * DIL_STEPS, (sb + 1) * DIL_STEPS)
                kcat, vcat = k_ref[krows, hs], v_ref[krows, hs]
            s = _dot_nt(q_ref[rows, hs], kcat) + (bias_first if sb == 0 else bias)
            mx = jnp.max(s, axis=-1, keepdims=True)
            p = jnp.exp(s - mx).astype(BF16)
            pv = _dot(p, jnp.concatenate([vcat, ones], axis=1))
            den = pv[:, DIL_HD:]
            o_ref[rows, hs] = (pv[:, :DIL_HD] / den).astype(BF16)
            lse_tile = jnp.where(lane == h, mx + jnp.log(den), lse_tile)
        lse_ref[rows, :] = lse_tile


def _dil_prompt(qkv, rb):
    dil, l, _ = qkv.shape
    nprev = rb // DIL_STEPS
    own = lambda c: pl.BlockSpec((None, rb, CW), lambda r, i: (r, i, c))
    prev = lambda c: pl.BlockSpec((None, DIL_STEPS, CW), lambda r, i: (r, jnp.maximum(i * nprev - 1, 0), c))
    return pl.pallas_call(
        _dil_prompt_kernel,
        grid=(dil, l // rb),
        in_specs=[own(0), own(1), prev(1), own(2), prev(2)],
        out_specs=[pl.BlockSpec((None, rb, CW), lambda r, i: (r, i, 0)),
                   pl.BlockSpec((None, rb, LANES), lambda r, i: (r, i, 0))],
        out_shape=[jax.ShapeDtypeStruct((dil, l, CW), BF16),
                   jax.ShapeDtypeStruct((dil, l, LANES), F32)],
        compiler_params=_params(("parallel", "parallel")),
        name=f"dil_prompt_d{dil}",
    )(qkv, qkv, qkv, qkv, qkv)


NUNIT = DEC_T * DIL_HEADS
KEYS = DIL_STEPS * 2 * DIL_HEADS + LANES


def _dil_sample_kernel(q_ref, kvn_ref, c0_ref, c1_ref, c2_ref, a0_ref, a1_ref, a2_ref,
                       o_ref, n0_ref, n1_ref, n2_ref, *, bb, tvalid):
    del a0_ref, a1_ref, a2_ref
    rpt = 2 * DIL_HEADS
    nrow = DIL_STEPS * rpt
    qall = q_ref[...]
    kvn = kvn_ref[...]
    row_u = lax.broadcasted_iota(jnp.int32, (NUNIT, KEYS), 0)
    key = lax.broadcasted_iota(jnp.int32, (NUNIT, KEYS), 1)
    t_u = row_u // DIL_HEADS
    mine = key % rpt == row_u % DIL_HEADS
    tok = jnp.where(key < nrow, key // rpt, (key - nrow) // rpt)
    bias_d1 = jnp.where(mine, jnp.where(key < nrow, jnp.where(tok >= t_u, 0.0, NEG),
                                        jnp.where(tok <= t_u, 0.0, NEG)), NEG)
    bias_dn = jnp.where(mine, jnp.where(key < nrow, 0.0, jnp.where(tok == t_u, 0.0, NEG)), NEG)
    t_row = lax.broadcasted_iota(jnp.int32, (NUNIT, DIL_HD), 0) // DIL_HEADS
    t_col = lax.broadcasted_iota(jnp.int32, (NUNIT, 1), 0) // DIL_HEADS
    c_refs = (c0_ref, c1_ref, c2_ref)
    n_refs = (n0_ref, n1_ref, n2_ref)

    passes = []
    scores = []
    for b in range(bb):
        for g, (_, dil) in enumerate(DIL_PAIRS):
            c_ref = c_refs[g]
            new_rows = []
            for t in range(tvalid):
                r = b * SROWS + t
                blk = jnp.concatenate(
                    [kvn[r:r + 1, g * 2 * CW + j * DIL_HD:g * 2 * CW + (j + 1) * DIL_HD] for j in range(rpt)],
                    axis=0)
                n_refs[g][b, t * rpt:(t + 1) * rpt, :] = blk
                new_rows.append(blk)
            new_rows.append(jnp.zeros((KEYS - nrow - tvalid * rpt, DIL_HD), F32))
            q_rows = jnp.concatenate(
                [qall[b * SROWS + t:b * SROWS + t + 1, g * CW + h * DIL_HD:g * CW + (h + 1) * DIL_HD]
                 for t in range(DEC_T) for h in range(DIL_HEADS)], axis=0)
            for t in (range(DEC_T) if dil > 1 else (None,)):
                rows = c_ref[b, :, 0 if t is None else t].reshape(nrow, DIL_HD)
                kv = jnp.concatenate([rows] + new_rows, axis=0).astype(BF16)
                q = q_rows if t is None else jnp.where(t_row == t, q_rows, 0.0)
                scores.append(_dot_nt(q.astype(BF16), kv) + (bias_d1 if t is None else bias_dn))
                passes.append((b, g, t, kv))
    s = jnp.concatenate(scores, axis=0)
    mx = jnp.max(s, axis=-1, keepdims=True)
    p = jnp.exp(s - mx)
    den = jnp.sum(p, axis=-1, keepdims=True)
    lse = mx + jnp.log(den)
    pv = pltpu.roll(p, DIL_HEADS, axis=1).astype(BF16)

    outs = [[jnp.zeros((NUNIT, DIL_HD), F32)] * N_GROUPS for _ in range(bb)]
    lses = [[jnp.zeros((NUNIT, 1), F32)] * N_GROUPS for _ in range(bb)]
    for i, (b, g, t, kv) in enumerate(passes):
        rs = slice(i * NUNIT, (i + 1) * NUNIT)
        o = _dot(pv[rs], kv) / den[rs]
        if t is None:
            outs[b][g], lses[b][g] = o, lse[rs]
        else:
            outs[b][g] = jnp.where(t_row == t, o, outs[b][g])
            lses[b][g] = jnp.where(t_col == t, lse[rs], lses[b][g])
    for b in range(bb):
        lb, ob = lses[b], outs[b]
        mxl = jnp.maximum(jnp.maximum(lb[0], lb[1]), lb[2])
        es = [jnp.exp(l - mxl) for l in lb]
        inv = 1.0 / (es[0] + es[1] + es[2])
        merged = (es[0] * inv) * ob[0] + (es[1] * inv) * ob[1] + (es[2] * inv) * ob[2]
        orow = [jnp.concatenate([merged[t * DIL_HEADS + h:t * DIL_HEADS + h + 1, :] for h in range(DIL_HEADS)],
                                axis=1) for t in range(DEC_T)]
        o_ref[b * SROWS:(b + 1) * SROWS, :] = jnp.concatenate(
            orow + [jnp.zeros((SROWS - DEC_T, DIL_OUT), F32)], axis=0)


def _dil_sample(qf, kvn, caches, rolled, layer, tvalid, bb):
    assert tvalid == DEC_T
    m = qf.shape[0]
    nb = caches[0].shape[1]
    rows = bb * SROWS
    rpt = 2 * DIL_HEADS
    views, cspecs, nspecs = [], [], []
    for c, (_, dil) in zip(caches, DIL_PAIRS):
        depth, _, wr, _ = c.shape
        views.append(c.reshape(depth, nb, DIL_STEPS, dil, rpt, LANES))
        cspecs.append(pl.BlockSpec((None, bb, DIL_STEPS, min(dil, DEC_T), rpt, LANES),
                                   lambda i: (layer, i, 0, 0, 0, 0)))
        nspecs.append(pl.BlockSpec((None, bb, tvalid * rpt, LANES),
                                   lambda i, last=wr // (tvalid * rpt) - 1: (layer, i, last, 0)))
    anyspec = pl.BlockSpec(memory_space=pl.ANY)
    outs = pl.pallas_call(
        functools.partial(_dil_sample_kernel, bb=bb, tvalid=tvalid),
        grid=(nb // bb,),
        in_specs=[pl.BlockSpec((rows, N_GROUPS * CW), lambda i: (i, 0)),
                  pl.BlockSpec((rows, N_GROUPS * 2 * CW), lambda i: (i, 0))] + cspecs + [anyspec] * N_GROUPS,
        out_specs=[pl.BlockSpec((rows, CW), lambda i: (i, 0))] + nspecs,
        out_shape=[jax.ShapeDtypeStruct((m, CW), F32)] + [jax.ShapeDtypeStruct(r.shape, F32) for r in rolled],
        input_output_aliases={2 + N_GROUPS + g: 1 + g for g in range(N_GROUPS)},
        compiler_params=_params(("parallel",)),
        name="dil_sample",
    )(qf, kvn, *views, *rolled)
    return outs[0], list(outs[1:])


def _tokens_from_streams(ref, scr):
    dil, n, w = ref.shape
    if dil == 1:
        return ref[0].astype(F32)
    nslab = w // LANES
    for r in range(dil):
        blk = ref[r].astype(F32)
        for j in range(nslab):
            scr[j, pl.ds(r, n, stride=dil), :] = blk[:, j * LANES:(j + 1) * LANES]
    return jnp.concatenate([scr[j] for j in range(nslab)], axis=1)


def _merge_kernel(*refs, grouped):
    if grouped:
        (x_ref, oa_ref, r_ref, ga_ref, gb_ref, o0_ref, o1_ref, o2_ref, l0_ref, l1_ref, l2_ref,
         gn_ref, wa_ref, wb_ref, wo_ref, nw_ref, h_ref, o_scr, l_scr) = refs
    else:
        x_ref, oa_ref, r_ref, ga_ref, gb_ref, ob_ref, gn_ref, wa_ref, wb_ref, wo_ref, nw_ref, h_ref = refs
    oa = oa_ref[...].astype(F32)
    gn = gn_ref[...]
    parts = [_rms(oa[:, h * GLA_DV:(h + 1) * GLA_DV], gn) for h in range(GLA_HEADS)]
    r = r_ref[...].astype(F32)
    oan = (jnp.concatenate(parts, axis=1) * (r * jax.nn.sigmoid(r))).astype(BF16)

    if grouped:
        lses = [_tokens_from_streams(l, l_scr) for l in (l0_ref, l1_ref, l2_ref)]
        mx = jnp.maximum(jnp.maximum(lses[0], lses[1]), lses[2])
        es = [jnp.exp(l - mx) for l in lses]
        inv = 1.0 / (es[0] + es[1] + es[2])
        outs = [_tokens_from_streams(o, o_scr) for o in (o0_ref, o1_ref, o2_ref)]
        ob = []
        for h in range(DIL_HEADS):
            hs = slice(h * DIL_HD, (h + 1) * DIL_HD)
            acc = None
            for e, o in zip(es, outs):
                term = (e * inv)[:, h:h + 1] * o[:, hs]
                acc = term if acc is None else acc + term
            ob.append(acc)
        ob = jnp.concatenate(ob, axis=1).astype(BF16)
    else:
        ob = ob_ref[...].astype(BF16)

    merged = (ga_ref[...].astype(F32) * _dot(oan, wa_ref[...])
              + gb_ref[...].astype(F32) * _dot(ob, wb_ref[...]))
    y = _dot(merged.astype(BF16), wo_ref[...])
    h_ref[...] = x_ref[...] + _rms(y, nw_ref[...])


def _merge(x, oa, z, dil_o, dil_lse, gn, wa, wb, wo, nw, tm):
    m, d = x.shape
    va = oa.shape[1]
    grouped = dil_lse is not None
    zc = lambda c: pl.BlockSpec((tm, d), lambda i: (i, c * CW // d))
    row = lambda w: pl.BlockSpec((tm, w), lambda i: (i, 0))
    stream = lambda a: pl.BlockSpec((a.shape[0], tm // a.shape[0], a.shape[2]), lambda i: (0, i, 0))
    if grouped:
        dil_specs = [stream(a) for a in dil_o] + [stream(a) for a in dil_lse]
        dil_args = [*dil_o, *dil_lse]
        scratch = [pltpu.VMEM((CW // LANES, tm, LANES), F32), pltpu.VMEM((1, tm, LANES), F32)]
    else:
        dil_specs, dil_args, scratch = [row(CW)], [dil_o], []
    return pl.pallas_call(
        functools.partial(_merge_kernel, grouped=grouped),
        grid=(m // tm,),
        in_specs=[row(d), row(va), zc(Z_RA), zc(Z_GA), zc(Z_GB)] + dil_specs
        + [_resident(gn.shape), _resident(wa.shape), _resident(wb.shape),
           _resident(wo.shape), _resident(nw.shape)],
        out_specs=row(d),
        out_shape=jax.ShapeDtypeStruct((m, d), F32),
        scratch_shapes=scratch,
        compiler_params=_params(("parallel",)),
        name="merge",
    )(x, oa, z, z, z, *dil_args, gn, wa, wb, wo, nw)


def _gelu_tanh(x):
    c = 0.7978845608028654
    hx = 0.5 * x
    return hx + hx * jnp.tanh(x * (c + (c * 0.044715) * (x * x)))


def _ffn_kernel(*refs, fc, sample):
    if sample:
        h_ref, nw1_ref, wup_ref, wc_ref, bc_ref, wdn_ref, nw2_ref, p1_ref, p2_ref, out_ref, gst_ref = refs
    else:
        h_ref, nw1_ref, wup_ref, wc_ref, bc_ref, wdn_ref, nw2_ref, out_ref, gst_ref, cbuf = refs
    tm = h_ref.shape[0]
    dff = wdn_ref.shape[0]
    i = pl.program_id(0)

    if not sample:
        @pl.when(i == 0)
        def _():
            cbuf[...] = jnp.zeros_like(cbuf)

    h = h_ref[...]
    hn = _rms(h, nw1_ref[...]).astype(BF16)
    if sample:
        tpos = lax.broadcasted_iota(jnp.int32, (tm, 1), 0) % SROWS
    else:
        row8 = lax.broadcasted_iota(jnp.int32, (SUBLANES, 1), 0)
    acc = None
    for c in range(dff // fc):
        cs = slice(c * fc, (c + 1) * fc)
        gp = _dot(hn, wup_ref[:, cs])
        val = _dot(hn, wup_ref[:, dff + c * fc:dff + (c + 1) * fc])
        g1 = pltpu.roll(gp, 1, axis=0)
        g2 = pltpu.roll(gp, 2, axis=0)
        if sample:
            g1 = jnp.where(tpos == 0, p1_ref[:, cs], g1)
            g2 = jnp.where(tpos < 2, p2_ref[:, cs], g2)
            gst_ref[:, cs] = gp
        else:
            carry = cbuf[:, cs]
            head1 = jnp.where(row8 < 1, pltpu.roll(carry, 1, axis=0), g1[0:SUBLANES])
            head2 = jnp.where(row8 < 2, pltpu.roll(carry, 2, axis=0), g2[0:SUBLANES])
            g1 = jnp.concatenate([head1, g1[SUBLANES:]], axis=0)
            g2 = jnp.concatenate([head2, g2[SUBLANES:]], axis=0)
            cbuf[:, cs] = gp[tm - SUBLANES:]
        conv = g2 * wc_ref[0:1, cs] + g1 * wc_ref[1:2, cs] + gp * wc_ref[2:3, cs] + bc_ref[:, cs]
        act = (_gelu_tanh(conv) * val).astype(BF16)
        part = _dot(act, wdn_ref[cs, :])
        acc = part if acc is None else acc + part
    if not sample:
        gst_ref[...] = cbuf[...]
    out_ref[...] = h + _rms(acc, nw2_ref[...])


def _ffn(h, nw1, wup, wc, bc, wdn, nw2, prev, tm, fc):
    m, d = h.shape
    dff = wdn.shape[0]
    sample = prev is not None
    row = lambda w: pl.BlockSpec((tm, w), lambda i: (i, 0))
    in_specs = [row(d), _resident((1, d)), _resident(wup.shape), _resident(wc.shape),
                _resident((1, dff)), _resident(wdn.shape), _resident((1, d))]
    args = [h, nw1, wup, wc, bc, wdn, nw2]
    out_specs, out_shape, scratch = [row(d)], [jax.ShapeDtypeStruct((m, d), F32)], []
    if sample:
        in_specs += [row(dff), row(dff)]
        args += list(prev)
        out_specs.append(row(dff))
        out_shape.append(jax.ShapeDtypeStruct((m, dff), F32))
    else:
        out_specs.append(pl.BlockSpec((SUBLANES, dff), lambda i: (i, 0)))
        out_shape.append(jax.ShapeDtypeStruct((m // tm * SUBLANES, dff), F32))
        scratch.append(pltpu.VMEM((SUBLANES, dff), F32))
    return pl.pallas_call(
        functools.partial(_ffn_kernel, fc=fc, sample=sample),
        grid=(m // tm,),
        in_specs=in_specs,
        out_specs=out_specs,
        out_shape=out_shape,
        scratch_shapes=scratch,
        compiler_params=_params(("arbitrary",)),
        name="ffn_sample" if sample else "ffn_prompt",
    )(*args)


def _rope_tables(pos):
    inv = ROPE_THETA ** (-jnp.arange(ROPE_HALF, dtype=F32) * 2.0 / ROPE_DIM)
    ang = pos[:, None] * inv[None, :]
    cos, sin = jnp.cos(ang), jnp.sin(ang)
    pad = LANES - ROPE_DIM
    one = jnp.ones((pos.shape[0], pad), F32)
    cos_t = jnp.concatenate([cos, cos, one], axis=1)
    sin_t = jnp.concatenate([-sin, sin, 0.0 * one], axis=1)
    return cos_t, sin_t


def _prep_in(w_in_l, d):
    qk = GLA_HEADS * GLA_DK
    va = GLA_HEADS * GLA_DV
    nb = N_GROUPS * DIL_OUT
    sizes = (qk, qk, va, va, GLA_RANK, 3 * nb, 2 * d)
    offs = np.cumsum((0,) + sizes)
    q_a, k_a, v_a, r_a, a_low, qkv_b, gate = (w_in_l[:, offs[j]:offs[j + 1]] for j in range(7))
    q_b, k_b, v_b = qkv_b[:, :nb], qkv_b[:, nb:2 * nb], qkv_b[:, 2 * nb:]
    wz = jnp.concatenate([q_a, k_a, v_a, r_a, gate, q_b, k_b, v_b], axis=1).astype(BF16)
    wal = jnp.pad(a_low, ((0, 0), (0, LANES - GLA_RANK))).astype(BF16)
    kv = []
    for g in range(N_GROUPS):
        kv += [k_b[:, g * DIL_OUT:(g + 1) * DIL_OUT], v_b[:, g * DIL_OUT:(g + 1) * DIL_OUT]]
    wkv = jnp.concatenate(kv, axis=1).astype(BF16)
    return wz, wal, wkv


def kernel(x_prompt, x_sample, state_gla, cache_dil_w128, cache_dil_w512, cache_dil_w2048, state_ffn_conv,
           w_in, w_gate_up, b_gate, gla_norm, w_branch_a, w_branch_b, w_out, norm_pre_mix, norm_post_mix,
           norm_pre_ffn, norm_post_ffn, w_ffn_up, w_conv, b_conv, w_ffn_down):
    depth, d = w_in.shape[0], w_in.shape[1]
    _, t_p, _ = x_prompt.shape
    nb, t_s, _ = x_sample.shape
    dff = w_ffn_down.shape[1]
    caches = (cache_dil_w128, cache_dil_w512, cache_dil_w2048)
    assert x_prompt.shape[0] == 1 and t_s <= SROWS // 2 and d == 2 * DIL_OUT
    for c, (win, _) in zip(caches, DIL_PAIRS):
        assert c.shape[2] == win and t_p >= win
    dils = tuple(dl for _, dl in DIL_PAIRS)
    tm = 256
    tm_big = min(512, t_p)
    m_s = nb * SROWS
    tm_s = min(tm, m_s)
    tm_ffn_s = min(128, m_s)

    xp = x_prompt.reshape(t_p, d)
    xs = jnp.pad(x_sample, ((0, 0), (0, SROWS - t_s), (0, 0))).reshape(nb * SROWS, d)
    cos_p, sin_p = _rope_tables(jnp.arange(t_p, dtype=F32))
    pos_s = PAST_LEN + jnp.minimum(jnp.arange(SROWS), t_s - 1).astype(F32)
    cos_s, sin_s = (jnp.tile(a, (nb, 1)) for a in _rope_tables(pos_s))
    cache_v = [c.reshape(depth, nb, c.shape[2] * 2 * DIL_HEADS, DIL_HD) for c in caches]
    new_cache = None

    p_gla, p_conv, s_conv = [], [], []
    p_kv = s_gla = None
    wins = tuple(w for w, _ in DIL_PAIRS)
    row2 = lambda a: a.reshape(1, -1)
    for i in range(depth):
        wz, wal, wkv = _prep_in(w_in[i], d)
        wgu = jnp.pad(w_gate_up[i], ((0, LANES - GLA_RANK), (0, 0))).astype(BF16)
        bg = row2(b_gate[i])
        wa, wb, wo = (w.astype(BF16) for w in (w_branch_a[i], w_branch_b[i], w_out[i]))
        wup, wdn = w_ffn_up[i].astype(BF16), w_ffn_down[i].astype(BF16)
        n1, n2, n3, n4 = (row2(n[i]) for n in (norm_pre_mix, norm_post_mix, norm_pre_ffn, norm_post_ffn))
        gn = row2(gla_norm[i])
        wc = jnp.pad(w_conv[i], ((0, SUBLANES - CONV_W), (0, 0)))
        bc = row2(b_conv[i])

        z, la, *qkv = _inproj(xp, n1, wz, wal, wgu, bg, cos_p, sin_p, tm_big, dils)
        p_kv = _kvtail_prompt(xp, n1, wkv, cos_p, sin_p, tm, wins, p_kv, i, depth)
        oa, sfin, new_cache = _gla_prompt(z, la, cache_v, new_cache, i, t_s * 2 * DIL_HEADS)
        dil = [_dil_prompt(a, min(DIL_ROWS_PER_STEP, a.shape[1])) for a in qkv]
        h = _merge(xp, oa, z, [o for o, _ in dil], [l for _, l in dil], gn, wa, wb, wo, n2, tm_big)
        xp, gst = _ffn(h, n3, wup, wc, bc, wdn, n4, None, tm_big, 1024)
        p_gla.append(sfin[None])
        p_conv.append(gst[-(CONV_W - 1):][None])

        z, la, *qkv = _inproj(xs, n1, wz, wal, wgu, bg, cos_s, sin_s, tm_s, (1,) * N_GROUPS)
        kvn = _kvtail(xs, n1, wkv, cos_s, sin_s, tm_s)
        oa, s_gla = _gla_sample(z, la, state_gla, s_gla, i, t_s)
        qf = jnp.concatenate([a[0, :, :CW] for a in qkv], axis=1).astype(F32)
        ob, new_cache = _dil_sample(qf, kvn, cache_v, new_cache, i, t_s, min(2, nb))
        h = _merge(xs, oa, z, ob, None, gn, wa, wb, wo, n2, tm_s)
        st = state_ffn_conv[i]
        zpad = lambda a: jnp.pad(a, ((0, 0), (0, SROWS - a.shape[1]), (0, 0))).reshape(m_s, dff)
        prev = (zpad(st[:, 1:2]), zpad(st))
        xs, gpre = _ffn(h, n3, wup, wc, bc, wdn, n4, prev, tm_ffn_s, 512)
        s_conv.append(gpre.reshape(nb, SROWS, dff)[:, t_s - (CONV_W - 1):t_s])

    y_p = xp.reshape(1, t_p, d)
    y_s = xs.reshape(nb, SROWS, d)[:, :t_s]
    p_caches = [kv.reshape(depth, 1, win, 2, DIL_HEADS, DIL_HD) for kv, win in zip(p_kv, wins)]
    s_caches = [c.reshape(o.shape) for c, o in zip(new_cache, caches)]
    return (y_p, y_s, jnp.stack(p_gla), *p_caches, jnp.stack(p_conv),
            s_gla, *s_caches, jnp.stack(s_conv))
```
